```python
import math
import jax, jax.numpy as jnp
from jax import lax
import numpy as np

D_MODEL = 1024
BATCH = 8
SEQ = 2048
DEPTH = 2
DEC_BATCH = 32
DEC_SEQ = 4
PAST_LEN = 8192
PAGE_SIZE = 128

HEAD_DIM = 64
MIX_W = D_MODEL
N_HEADS = MIX_W // HEAD_DIM
GLA_HEADS = N_HEADS // 4
GM_HEADS = N_HEADS // 4
SB_HEADS = N_HEADS - GLA_HEADS - GM_HEADS
GLA_W = GLA_HEADS * HEAD_DIM
GM_W = GM_HEADS * HEAD_DIM
SB_W = SB_HEADS * HEAD_DIM
GLA_LOWRANK = 16
GLA_TAU = 16.0
GLA_CHUNK = 64
GM_CHUNK = 128
SB_BLOCK = 128
SB_BIAS_INIT = -6.0
D_FF = ((8 * D_MODEL // 3 + 127) // 128) * 128
N_COND = 9
MACARON_W = 0.5
EPS = 1e-6
IN_COLS = 4 * GLA_W + GLA_LOWRANK + 2 * GM_W + 3 * SB_W

kernel_name = "hymba_gla_gmlp_stickbreak_macaron_decode"


def rmsnorm(x, w):
    xf = x.astype(jnp.float32)
    y = xf * lax.rsqrt(jnp.mean(xf * xf, axis=-1, keepdims=True) + EPS)
    return (y * w.astype(jnp.float32)).astype(x.dtype)


def swiglu(a, w_in, w_out):
    g, u = jnp.split(a @ w_in, 2, axis=-1)
    return (jax.nn.silu(g) * u) @ w_out


def split_cols(h):
    sizes = [GLA_W, GLA_W, GLA_W, GLA_W, GLA_LOWRANK, GM_W, GM_W, SB_W, SB_W, SB_W]
    idx, acc = [], 0
    for s in sizes[:-1]:
        acc += s
        idx.append(acc)
    return jnp.split(h, idx, axis=-1)


def gla_chunked(q, k, v, log_a, s0):
    B, L, H, DK = q.shape
    C = GLA_CHUNK if L % GLA_CHUNK == 0 else L
    n = L // C

    def to_chunks(t):
        return t.astype(jnp.float32).reshape(B, n, C, H, t.shape[-1]).transpose(1, 0, 3, 2, 4)

    qc, kc, vc, ac = (to_chunks(q * (DK ** -0.5)), to_chunks(k), to_chunks(v), to_chunks(log_a))
    causal = jnp.tril(jnp.ones((C, C), dtype=bool))

    def step(S, inp):
        qi, ki, vi, ai = inp
        b = jnp.cumsum(ai, axis=-2)
        q_t = qi * jnp.exp(b)
        k_t = ki * jnp.exp(-b)
        att = jnp.where(causal, jnp.einsum('bhtk,bhsk->bhts', q_t, k_t), 0.0)
        o = jnp.einsum('bhtk,bhkv->bhtv', q_t, S) + jnp.einsum('bhts,bhsv->bhtv', att, vi)
        b_last = b[:, :, -1:, :]
        k_end = ki * jnp.exp(b_last - b)
        S_new = S * jnp.exp(b_last[:, :, 0, :, None]) + jnp.einsum('bhsk,bhsv->bhkv', k_end, vi)
        return S_new, o

    S, o = lax.scan(step, s0.astype(jnp.float32), (qc, kc, vc, ac))
    o = o.transpose(1, 0, 3, 2, 4).reshape(B, L, H, v.shape[-1])
    return o, S


def gla_mixer(q, k, v, g, lr, w_gate2, b_gate, norm_w, s0):
    B, L, _ = q.shape
    log_a = jax.nn.log_sigmoid((lr @ w_gate2 + b_gate).astype(jnp.float32)) / GLA_TAU
    hs = lambda t: t.reshape(B, L, GLA_HEADS, HEAD_DIM)
    o, S = gla_chunked(hs(q), hs(k), hs(v), hs(log_a), s0)
    o = rmsnorm(o.astype(q.dtype), norm_w.reshape(GLA_HEADS, HEAD_DIM))
    return o.reshape(B, L, GLA_W) * jax.nn.silu(g), S


def gmlp_mixer(u, v, ln_w, ln_b, w_s, b_s):
    B, L, _ = v.shape
    vf = v.astype(jnp.float32)
    mu = jnp.mean(vf, axis=-1, keepdims=True)
    var = jnp.mean((vf - mu) ** 2, axis=-1, keepdims=True)
    vn = ((vf - mu) * lax.rsqrt(var + EPS) * ln_w.astype(jnp.float32)
          + ln_b.astype(jnp.float32)).astype(v.dtype)
    n = -(-L // GM_CHUNK)
    pad = n * GM_CHUNK - L
    vp = jnp.pad(vn, ((0, 0), (0, pad), (0, 0))).reshape(B, n, GM_CHUNK, GM_HEADS, HEAD_DIM)
    w = jnp.where(jnp.tril(jnp.ones((GM_CHUNK, GM_CHUNK), dtype=bool)), w_s, 0.0)
    s = jnp.einsum('hts,bnshd->bnthd', w, vp) + b_s.T[None, None, :, :, None]
    s = s.reshape(B, n * GM_CHUNK, GM_W)[:, :L]
    return u * s, vn


def sb_block(q, k, v, q_pos, bias):
    z = (jnp.einsum('bthd,bshd->bhts', q, k).astype(jnp.float32) * (HEAD_DIM ** -0.5)
         + bias.astype(jnp.float32)[None, :, None, None])
    kpos = jnp.arange(k.shape[1])
    mask = kpos[None, :] < q_pos[:, None]
    log1m = jnp.where(mask, -jax.nn.softplus(z), 0.0)
    suffix = lax.cumsum(log1m, axis=3, reverse=True) - log1m
    A = jnp.where(mask, jnp.exp(jax.nn.log_sigmoid(z) + suffix), 0.0)
    return jnp.einsum('bhts,bshd->bthd', A.astype(v.dtype), v)


def sb_attend(q, k, v, pos0, bias):
    T = q.shape[1]
    outs = []
    for start in range(0, T, SB_BLOCK):
        end = min(start + SB_BLOCK, T)
        kl = pos0 + end
        outs.append(sb_block(q[:, start:end], k[:, :kl], v[:, :kl],
                             pos0 + jnp.arange(start, end), bias))
    return jnp.concatenate(outs, axis=1)


def gather_pages(cache, page_table):
    g = cache[page_table]
    return g.reshape(page_table.shape[0], -1, SB_HEADS, HEAD_DIM)


def mixer(a, w_in, w_out, gla_w2, gla_b, gla_nw, gm_ln_w, gm_ln_b, gm_ws, gm_bs, sb_b,
          gla_s0, past_k, past_v):
    B, L, _ = a.shape
    gq, gk, gv, gg, glr, mu, mv, sq, sk, sv = split_cols(a @ w_in)
    o_gla, s_gla = gla_mixer(gq, gk, gv, gg, glr, gla_w2, gla_b, gla_nw, gla_s0)
    o_gm, v_gm = gmlp_mixer(mu, mv, gm_ln_w, gm_ln_b, gm_ws, gm_bs)
    hs = lambda t: t.reshape(B, L, SB_HEADS, HEAD_DIM)
    k_new, v_new = hs(sk), hs(sv)
    if past_k is None:
        kk, vv, pos0 = k_new, v_new, 0
    else:
        kk = jnp.concatenate([past_k.astype(k_new.dtype), k_new], axis=1)
        vv = jnp.concatenate([past_v.astype(v_new.dtype), v_new], axis=1)
        pos0 = past_k.shape[1]
    o_sb = sb_attend(hs(sq), kk, vv, pos0, sb_b).reshape(B, L, SB_W)
    o = jnp.concatenate([o_gla, o_gm, o_sb], axis=-1) @ w_out
    return o, s_gla, v_gm, k_new, v_new


def decoder_layer(x, c, cond_w, cond_b, norm_pre, norm_post, ffn_w_in, ffn_w_out, mix_w_in,
                  mix_w_out, gla_w2, gla_b, gla_nw, gm_ln_w, gm_ln_b, gm_ws, gm_bs, sb_b,
                  gla_s0, past_k, past_v):
    m = (jax.nn.silu(c) @ cond_w + cond_b).reshape(c.shape[0], N_COND, 1, D_MODEL)

    def pre(i, h):
        return rmsnorm(h, norm_pre[i]) * (1.0 + m[:, 3 * i + 1]) + m[:, 3 * i]

    def post(i, h, y, rw):
        return h + rw * m[:, 3 * i + 2] * rmsnorm(y, norm_post[i])

    x = post(0, x, swiglu(pre(0, x), ffn_w_in[0], ffn_w_out[0]), MACARON_W)
    o, s_gla, v_gm, k_new, v_new = mixer(pre(1, x), mix_w_in, mix_w_out, gla_w2, gla_b, gla_nw,
                                         gm_ln_w, gm_ln_b, gm_ws, gm_bs, sb_b, gla_s0,
                                         past_k, past_v)
    x = post(1, x, o, 1.0)
    x = post(2, x, swiglu(pre(2, x), ffn_w_in[1], ffn_w_out[1]), MACARON_W)
    return x, s_gla, v_gm, k_new, v_new


def setup_inputs(seed: int = 0) -> dict:
    key = jax.random.key(seed)
    ks = jax.random.split(key, 26)
    f32 = jnp.float32
    nrm = lambda k, shape, s: jax.random.normal(k, shape, f32) * s
    n_pages = PAST_LEN // PAGE_SIZE
    n_used = DEC_BATCH * n_pages
    n_pool = n_used + max(1, n_used // 4)
    perm = jax.random.permutation(ks[0], n_pool)
    page_table = perm[:n_used].reshape(DEC_BATCH, n_pages).astype(jnp.int32)
    return {
        "x_prompt": nrm(ks[1], (BATCH, SEQ, D_MODEL), 1.0),
        "x_sample": nrm(ks[2], (DEC_BATCH, DEC_SEQ, D_MODEL), 1.0),
        "c_prompt": nrm(ks[3], (BATCH, D_MODEL), 1.0),
        "c_sample": nrm(ks[4], (DEC_BATCH, D_MODEL), 1.0),
        "cache_sb_k": nrm(ks[5], (DEPTH, n_pool, PAGE_SIZE, SB_HEADS, HEAD_DIM), 1.0),
        "cache_sb_v": nrm(ks[6], (DEPTH, n_pool, PAGE_SIZE, SB_HEADS, HEAD_DIM), 1.0),
        "state_gla": nrm(ks[7], (DEPTH, DEC_BATCH, GLA_HEADS, HEAD_DIM, HEAD_DIM), 1.0),
        "page_table": page_table,
        "cond_w": nrm(ks[8], (DEPTH, D_MODEL, N_COND * D_MODEL), 0.5 * D_MODEL ** -0.5),
        "cond_b": nrm(ks[9], (DEPTH, N_COND * D_MODEL), 0.02),
        "norm_pre": 1.0 + nrm(ks[10], (DEPTH, 3, D_MODEL), 0.05),
        "norm_post": 1.0 + nrm(ks[11], (DEPTH, 3, D_MODEL), 0.05),
        "ffn_w_in": nrm(ks[12], (DEPTH, 2, D_MODEL, 2 * D_FF), D_MODEL ** -0.5),
        "ffn_w_out": nrm(ks[13], (DEPTH, 2, D_FF, D_MODEL), D_FF ** -0.5),
        "mix_w_in": nrm(ks[14], (DEPTH, D_MODEL, IN_COLS), D_MODEL ** -0.5),
        "mix_w_out": nrm(ks[15], (DEPTH, MIX_W, D_MODEL), MIX_W ** -0.5),
        "gla_w_gate2": nrm(ks[16], (DEPTH, GLA_LOWRANK, GLA_W), GLA_LOWRANK ** -0.5),
        "gla_b_gate": nrm(ks[17], (DEPTH, GLA_W), 0.1),
        "gla_norm": 1.0 + nrm(ks[18], (DEPTH, GLA_W), 0.05),
        "gm_ln_w": 1.0 + nrm(ks[19], (DEPTH, GM_W), 0.05),
        "gm_ln_b": nrm(ks[20], (DEPTH, GM_W), 0.02),
        "gm_ws": nrm(ks[21], (DEPTH, GM_HEADS, GM_CHUNK, GM_CHUNK), GM_CHUNK ** -0.5),
        "gm_bs": 1.0 + nrm(ks[22], (DEPTH, GM_HEADS, GM_CHUNK), 0.1),
        "sb_bias": SB_BIAS_INIT + nrm(ks[23], (DEPTH, SB_HEADS), 0.5),
    }


def reference(x_prompt, x_sample, c_prompt, c_sample, cache_sb_k, cache_sb_v, state_gla,
              page_table, cond_w, cond_b, norm_pre, norm_post, ffn_w_in, ffn_w_out, mix_w_in,
              mix_w_out, gla_w_gate2, gla_b_gate, gla_norm, gm_ln_w, gm_ln_b, gm_ws, gm_bs,
              sb_bias):
    xp, xs = x_prompt, x_sample
    gla_p, gla_s, kp, vp, ksl, vsl, gms = [], [], [], [], [], [], []
    s0_prompt = jnp.zeros((x_prompt.shape[0], GLA_HEADS, HEAD_DIM, HEAD_DIM), jnp.float32)
    for l in range(DEPTH):
        w = (cond_w[l], cond_b[l], norm_pre[l], norm_post[l], ffn_w_in[l], ffn_w_out[l],
             mix_w_in[l], mix_w_out[l], gla_w_gate2[l], gla_b_gate[l], gla_norm[l],
             gm_ln_w[l], gm_ln_b[l], gm_ws[l], gm_bs[l], sb_bias[l])
        xp, sg, _, k_new, v_new = decoder_layer(xp, c_prompt, *w, s0_prompt, None, None)
        gla_p.append(sg)
        kp.append(k_new)
        vp.append(v_new)
        past_k = gather_pages(cache_sb_k[l], page_table)
        past_v = gather_pages(cache_sb_v[l], page_table)
        xs, sg2, gv, k2, v2 = decoder_layer(xs, c_sample, *w, state_gla[l], past_k, past_v)
        gla_s.append(sg2)
        ksl.append(k2)
        vsl.append(v2)
        gms.append(gv)
    return (xp, xs, jnp.stack(gla_p), jnp.stack(gla_s), jnp.stack(kp), jnp.stack(vp),
            jnp.stack(ksl), jnp.stack(vsl), jnp.stack(gms))
```

```python
import functools

import jax
import jax.numpy as jnp
from jax import lax
from jax.experimental import pallas as pl
from jax.experimental.pallas import tpu as pltpu

D_MODEL = 1024
HEAD_DIM = 64
GLA_HEADS = 4
GM_HEADS = 4
SB_HEADS = 8
GLA_W = GLA_HEADS * HEAD_DIM
GM_W = GM_HEADS * HEAD_DIM
SB_W = SB_HEADS * HEAD_DIM
GLA_LOWRANK = 16
GLA_TAU = 16.0
GLA_CHUNK = 64
GM_CHUNK = 128
PAGE_SIZE = 128
D_FF = 2816
N_COND = 9
MACARON_W = 0.5
EPS = 1e-6

LANES = 128
SUBLANES = 8
LR_PAD = LANES
COL_GLA = 0
COL_GM = 4 * GLA_W
COL_SBQ = COL_GM + 2 * GM_W
COL_LR = COL_SBQ + SB_W
MIX_COLS = COL_LR + LR_PAD

FFN_CHUNK = 256
TM_PROMPT = 512
SB_TQ = 256
GLA_TL = 256
SAMPLE_PAGES_PER_STEP = 8
VMEM_LIMIT = 48 * 1024 * 1024

_f32 = jnp.float32
_bf16 = jnp.bfloat16


def _bf(x):
    return x.astype(_bf16)


def _dot(a, b):
    return jnp.dot(a, b, preferred_element_type=_f32)


def _dot_nt(a, b):
    return lax.dot_general(a, b, (((1,), (1,)), ((), ())), preferred_element_type=_f32)


def _dot_tn(a, b):
    return lax.dot_general(a, b, (((0,), (0,)), ((), ())), preferred_element_type=_f32)


def _split_bf16(x, passes):
    pieces = []
    rem = x
    for _ in range(passes):
        piece = _bf(rem)
        pieces.append(piece)
        rem = rem - piece.astype(_f32)
    return pieces


def _dot_split(x, m, passes):
    return sum(_dot(p, m) for p in _split_bf16(x, passes))


def _cumsum_rows(x, tri):
    return sum(_dot(tri, p) for p in _split_bf16(x, 3))


def _rms(x, w):
    return x * lax.rsqrt(jnp.mean(x * x, axis=-1, keepdims=True) + EPS) * w


def _sigmoid(x):
    return 1.0 / (1.0 + jnp.exp(-x))


def _log1p_exp_neg_abs(z):
    return jnp.log1p(jnp.exp(-jnp.abs(z)))


def _div_pow2(x, d):
    shift = d.bit_length() - 1
    assert d == 1 << shift
    return lax.shift_right_logical(x, shift)


def _params(sem, vmem=None):
    return pltpu.CompilerParams(dimension_semantics=sem, vmem_limit_bytes=vmem)


def _resident(shape, index_map):
    return pl.BlockSpec(shape, index_map, pipeline_mode=pl.Buffered(1))


def _cond_kernel(c_ref, w_ref, b_ref, o_ref):
    c = c_ref[...]
    s = c * _sigmoid(c)
    o_ref[...] = _dot(_bf(s), _bf(w_ref[...])) + b_ref[...]


def _cond(c_all, cond_w, cond_b):
    depth = cond_w.shape[0]
    n = c_all.shape[0]
    tn = D_MODEL
    return pl.pallas_call(
        _cond_kernel,
        grid=(depth, N_COND * D_MODEL // tn),
        in_specs=[
            pl.BlockSpec((n, D_MODEL), lambda l, j: (0, 0)),
            pl.BlockSpec((None, D_MODEL, tn), lambda l, j: (l, 0, j)),
            pl.BlockSpec((None, 1, tn), lambda l, j: (l, 0, j)),
        ],
        out_specs=pl.BlockSpec((None, n, tn), lambda l, j: (l, 0, j)),
        out_shape=jax.ShapeDtypeStruct((depth, n, N_COND * D_MODEL), _f32),
        compiler_params=_params(("parallel", "parallel")),
        name="cond",
    )(c_all, cond_w, cond_b.reshape(depth, 1, N_COND * D_MODEL))


class _Cond:
    def __init__(self, m, seq_rows, tm):
        if seq_rows % tm == 0:
            self.arr = m.reshape(m.shape[0], 1, N_COND * D_MODEL)
            per = seq_rows // tm
            self._spec = lambda j: pl.BlockSpec((None, 1, D_MODEL), lambda i: (i // per, 0, j))
        else:
            assert tm % seq_rows == 0
            self.arr = jnp.repeat(m, seq_rows, axis=0)
            self._spec = lambda j: pl.BlockSpec((tm, D_MODEL), lambda i: (i, j))

    def spec(self, sub, kind):
        return self._spec(3 * sub + kind)


def _ffn_kernel(x_ref, sh_ref, sc_ref, gt_ref, npre_ref, npost_ref, win_ref, wout_ref, o_ref,
                a_ref, acc_ref):
    x = x_ref[...]
    a = _rms(x, npre_ref[...]) * (1.0 + sc_ref[...]) + sh_ref[...]
    a_ref[...] = _bf(a)
    for c in range(D_FF // FFN_CHUNK):
        lo = c * FFN_CHUNK
        ab = a_ref[...]
        g = _dot(ab, win_ref[:, lo:lo + FFN_CHUNK])
        u = _dot(ab, win_ref[:, D_FF + lo:D_FF + lo + FFN_CHUNK])
        act = _bf(g * _sigmoid(g) * u)
        y = _dot(act, wout_ref[lo:lo + FFN_CHUNK, :])
        if c == 0:
            acc_ref[...] = y
        else:
            acc_ref[...] += y
    o_ref[...] = x + MACARON_W * gt_ref[...] * _rms(acc_ref[...], npost_ref[...])


def _ffn(x, cond, sub, npre, npost, w_in, w_out, l, idx, tm):
    m = x.shape[0]
    row = lambda i: (i, 0)
    vec = pl.BlockSpec((1, D_MODEL), lambda i: (0, 0))
    return pl.pallas_call(
        _ffn_kernel,
        grid=(m // tm,),
        in_specs=[
            pl.BlockSpec((tm, D_MODEL), row),
            cond.spec(sub, 0), cond.spec(sub, 1), cond.spec(sub, 2),
            vec, vec,
            _resident((None, None, D_MODEL, 2 * D_FF), lambda i: (l, idx, 0, 0)),
            _resident((None, None, D_FF, D_MODEL), lambda i: (l, idx, 0, 0)),
        ],
        out_specs=pl.BlockSpec((tm, D_MODEL), row),
        out_shape=jax.ShapeDtypeStruct((m, D_MODEL), _f32),
        scratch_shapes=[pltpu.VMEM((tm, D_MODEL), _bf16), pltpu.VMEM((tm, D_MODEL), _f32)],
        compiler_params=_params(("parallel",), VMEM_LIMIT),
        name="ffn",
    )(x, cond.arr, cond.arr, cond.arr, npre, npost, w_in, w_out)


def _mixin_kernel(x_ref, sh_ref, sc_ref, npre_ref, w_ref, wt_ref, w2_ref, bg_ref, lnw_ref, lnb_ref,
                  ws_ref, bs_ref,
                  gla_ref, la_ref, ogm_ref, vn_ref, qb_ref, kt_ref, vt_ref, ktb_ref, vtb_ref,
                  *, period):
    tm = x_ref.shape[0]
    tk = ktb_ref.shape[-1]
    x = x_ref[...]
    a = _bf(_rms(x, npre_ref[...]) * (1.0 + sc_ref[...]) + sh_ref[...])

    gla_ref[...] = _dot(a, w_ref[:, COL_GLA:COL_GLA + 4 * GLA_W])
    lr = _dot(a, w_ref[:, COL_LR:COL_LR + LR_PAD])
    zl = _dot(_bf(lr), w2_ref[...]) + bg_ref[...]
    la_ref[...] = (jnp.minimum(zl, 0.0) - _log1p_exp_neg_abs(zl)) * (1.0 / GLA_TAU)

    q = _dot(a, w_ref[:, COL_SBQ:COL_SBQ + SB_W])
    qb_ref[...] = _bf(q * (HEAD_DIM ** -0.5))
    kt = _dot_nt(wt_ref[0:SB_W, :], a)
    vt = _dot_nt(wt_ref[SB_W:2 * SB_W, :], a)
    kt_ref[...] = kt
    vt_ref[...] = vt
    for j in range(tm // tk):
        ktb_ref[j] = _bf(kt[:, j * tk:(j + 1) * tk])
        vtb_ref[j] = _bf(vt[:, j * tk:(j + 1) * tk])

    mu = _dot(a, w_ref[:, COL_GM:COL_GM + GM_W])
    mv = _dot(a, w_ref[:, COL_GM + GM_W:COL_GM + 2 * GM_W])
    mean = jnp.mean(mv, axis=-1, keepdims=True)
    d = mv - mean
    var = jnp.mean(d * d, axis=-1, keepdims=True)
    vn = d * lax.rsqrt(var + EPS) * lnw_ref[...] + lnb_ref[...]
    vn_ref[...] = vn
    vnb = _bf(vn)
    r = lax.broadcasted_iota(jnp.int32, (GM_CHUNK, GM_CHUNK), 0)
    c = lax.broadcasted_iota(jnp.int32, (GM_CHUNK, GM_CHUNK), 1)
    keep = (c <= r) & (_div_pow2(r, period) == _div_pow2(c, period))
    wm = [_bf(jnp.where(keep, ws_ref[h], 0.0)) for h in range(GM_HEADS)]
    col_head = _div_pow2(lax.broadcasted_iota(jnp.int32, (GM_CHUNK, GM_W), 1), HEAD_DIM)
    for ci in range(tm // GM_CHUNK):
        rows = slice(ci * GM_CHUNK, (ci + 1) * GM_CHUNK)
        vc = vnb[rows, :]
        s = bs_ref[...]
        for h in range(GM_HEADS):
            s = s + jnp.where(col_head == h, _dot(wm[h], vc), 0.0)
        ogm_ref[rows, :] = _bf(mu[rows, :] * s)


def _mixin(x, cond, npre, w_nn, w_t, l, w2p, bg, lnw, lnb, ws, bs, tm, period, seq, tk):
    m = x.shape[0]
    n_seq, per = m // seq, seq // tm
    row = lambda i: (i, 0)
    const2 = lambda i: (0, 0)
    rows_out = [(4 * GLA_W, _f32), (GLA_W, _f32), (GM_W, _bf16), (GM_W, _f32), (SB_W, _bf16)]
    t_spec = pl.BlockSpec((None, SB_W, tm), lambda i: (i // per, 0, i % per))
    tb_spec = pl.BlockSpec((None, tm // tk, SB_W, tk), lambda i: (i // per, i % per, 0, 0))
    t_shape = jax.ShapeDtypeStruct((n_seq, SB_W, seq), _f32)
    tb_shape = jax.ShapeDtypeStruct((n_seq, seq // tk, SB_W, tk), _bf16)
    return pl.pallas_call(
        functools.partial(_mixin_kernel, period=period),
        grid=(m // tm,),
        in_specs=[
            pl.BlockSpec((tm, D_MODEL), row),
            cond.spec(1, 0), cond.spec(1, 1),
            pl.BlockSpec((1, D_MODEL), const2),
            _resident((None, D_MODEL, MIX_COLS), lambda i: (l, 0, 0)),
            _resident((None, 2 * SB_W, D_MODEL), lambda i: (l, 0, 0)),
            pl.BlockSpec((LR_PAD, GLA_W), const2),
            pl.BlockSpec((1, GLA_W), const2),
            pl.BlockSpec((1, GM_W), const2),
            pl.BlockSpec((1, GM_W), const2),
            pl.BlockSpec((GM_HEADS, GM_CHUNK, GM_CHUNK), lambda i: (0, 0, 0)),
            pl.BlockSpec((GM_CHUNK, GM_W), const2),
        ],
        out_specs=[pl.BlockSpec((tm, w), row) for w, _ in rows_out] + [t_spec, t_spec, tb_spec, tb_spec],
        out_shape=[jax.ShapeDtypeStruct((m, w), dt) for w, dt in rows_out]
        + [t_shape, t_shape, tb_shape, tb_shape],
        compiler_params=_params(("parallel",), VMEM_LIMIT),
        name="mixin",
    )(x, cond.arr, cond.arr, npre, w_nn, w_t, w2p, bg, lnw, lnb, ws, bs)


def _gla_kernel(gla_ref, la_ref, nw_ref, s0_ref, o_ref, s_ref, st_ref, ob_ref, *, chunk):
    tl = gla_ref.shape[0]
    j = pl.program_id(1)

    @pl.when(j == 0)
    def _():
        for h in range(GLA_HEADS):
            st_ref[h] = s0_ref[h].T

    ri = lax.broadcasted_iota(jnp.int32, (chunk, chunk), 0)
    ci = lax.broadcasted_iota(jnp.int32, (chunk, chunk), 1)
    causal = ci <= ri
    tri = _bf(jnp.where(causal, 1.0, 0.0))
    gi = _div_pow2(lax.broadcasted_iota(jnp.int32, (GLA_W, GLA_W), 0), HEAD_DIM)
    gj = _div_pow2(lax.broadcasted_iota(jnp.int32, (GLA_W, GLA_W), 1), HEAD_DIM)
    group = _bf(jnp.where(gi == gj, 1.0, 0.0))

    for c in range(tl // chunk):
        rows = slice(c * chunk, (c + 1) * chunk)
        q = gla_ref[rows, 0:GLA_W] * (HEAD_DIM ** -0.5)
        k = gla_ref[rows, GLA_W:2 * GLA_W]
        v = gla_ref[rows, 2 * GLA_W:3 * GLA_W]
        g = gla_ref[rows, 3 * GLA_W:4 * GLA_W]
        b = _cumsum_rows(la_ref[rows, :], tri)
        b_last = b[chunk - 1:chunk, :]
        q_t = q * jnp.exp(b)
        k_t = k * jnp.exp(-b)
        k_end = k * jnp.exp(b_last - b)
        decay = jnp.exp(b_last)
        for h in range(GLA_HEADS):
            hs = slice(h * HEAD_DIM, (h + 1) * HEAD_DIM)
            qh = _bf(q_t[:, hs])
            vh = _bf(v[:, hs])
            att = jnp.where(causal, _dot_nt(qh, _bf(k_t[:, hs])), 0.0)
            st = st_ref[h]
            ob_ref[rows, hs] = _dot_nt(qh, _bf(st)) + _dot(_bf(att), vh)
            st_ref[h] = st * decay[:, hs] + _dot_tn(vh, _bf(k_end[:, hs]))
        o = ob_ref[rows, :]
        ms = _dot_split(o * o, group, 2) * (1.0 / HEAD_DIM)
        y = o * lax.rsqrt(ms + EPS) * nw_ref[...]
        o_ref[rows, :] = _bf(y * (g * _sigmoid(g)))

    @pl.when(j == pl.num_programs(1) - 1)
    def _():
        for h in range(GLA_HEADS):
            s_ref[h] = st_ref[h].T


def _gla(gla, la, nw, s0, tl, chunk):
    bsz, length, _ = gla.shape
    state_spec = pl.BlockSpec((None, GLA_HEADS, HEAD_DIM, HEAD_DIM), lambda b, j: (b, 0, 0, 0))
    return pl.pallas_call(
        functools.partial(_gla_kernel, chunk=chunk),
        grid=(bsz, length // tl),
        in_specs=[
            pl.BlockSpec((None, tl, 4 * GLA_W), lambda b, j: (b, j, 0)),
            pl.BlockSpec((None, tl, GLA_W), lambda b, j: (b, j, 0)),
            pl.BlockSpec((1, GLA_W), lambda b, j: (0, 0)),
            state_spec,
        ],
        out_specs=[pl.BlockSpec((None, tl, GLA_W), lambda b, j: (b, j, 0)), state_spec],
        out_shape=[
            jax.ShapeDtypeStruct((bsz, length, GLA_W), _bf16),
            jax.ShapeDtypeStruct((bsz, GLA_HEADS, HEAD_DIM, HEAD_DIM), _f32),
        ],
        scratch_shapes=[
            pltpu.VMEM((GLA_HEADS, HEAD_DIM, HEAD_DIM), _f32),
            pltpu.VMEM((tl, GLA_W), _f32),
        ],
        compiler_params=_params(("parallel", "arbitrary")),
        name="gla",
    )(gla, la, nw, s0)


def _sb_block(z, vt, upper, c_run, mask):
    l = _log1p_exp_neg_abs(z)
    log1m = -(jnp.maximum(z, 0.0) + l)
    if mask is not None:
        log1m = jnp.where(mask, log1m, 0.0)
    suffix = _dot_split(log1m, upper, 2) + c_run
    a = jnp.exp(jnp.minimum(z, 0.0) - l + suffix)
    if mask is not None:
        a = jnp.where(mask, a, 0.0)
    return _dot_nt(_bf(a), vt), jnp.sum(log1m, axis=-1, keepdims=True)


def _sbp_kernel(bias_ref, q_ref, k_ref, v_ref, o_ref):
    tq = q_ref.shape[0]
    assert k_ref.shape[-1] == tq
    i = pl.program_id(1)
    r = lax.broadcasted_iota(jnp.int32, (tq, tq), 0)
    c = lax.broadcasted_iota(jnp.int32, (tq, tq), 1)
    upper = _bf(jnp.where(r > c, 1.0, 0.0))
    strict = c < r
    for h in range(SB_HEADS):
        hs = slice(h * HEAD_DIM, (h + 1) * HEAD_DIM)
        qh = q_ref[:, hs]
        bias = bias_ref[h]

        def block(j, c_run, acc, mask):
            z = _dot(qh, k_ref[j, hs, :]) + bias
            av, rs = _sb_block(z, v_ref[j, hs, :], upper, c_run, mask)
            return c_run + rs, acc + av

        c_run, acc = block(i, jnp.zeros((tq, 1), _f32), jnp.zeros((tq, HEAD_DIM), _f32), strict)
        c_run, acc = lax.fori_loop(
            0, i, lambda jj, carry: block(i - 1 - jj, carry[0], carry[1], None), (c_run, acc))
        o_ref[:, hs] = _bf(acc)


def _sb_prompt(qb, ktb, vtb, bias):
    bsz, length, _ = qb.shape
    nblk, _, tq = ktb.shape[1:]
    kv_spec = pl.BlockSpec((None, nblk, SB_W, tq), lambda b, i: (b, 0, 0, 0))
    return pl.pallas_call(
        _sbp_kernel,
        grid=(bsz, nblk),
        in_specs=[
            pl.BlockSpec(memory_space=pltpu.SMEM),
            pl.BlockSpec((None, tq, SB_W), lambda b, i: (b, i, 0)),
            kv_spec, kv_spec,
        ],
        out_specs=pl.BlockSpec((None, tq, SB_W), lambda b, i: (b, i, 0)),
        out_shape=jax.ShapeDtypeStruct((bsz, length, SB_W), _bf16),
        compiler_params=_params(("parallel", "arbitrary"), VMEM_LIMIT),
        name="sb_prompt",
    )(bias, qb, ktb, vtb)


def _sbs_kernel(pt_ref, bias_ref, qbd_ref, kn_ref, vn_ref, *refs, n_new):
    pp = SAMPLE_PAGES_PER_STEP
    k_refs, v_refs = refs[:pp], refs[pp:2 * pp]
    o_ref, c_ref, acc_ref = refs[2 * pp:]
    g = pl.program_id(1)
    rows = qbd_ref.shape[0]
    r = lax.broadcasted_iota(jnp.int32, (PAGE_SIZE, PAGE_SIZE), 0)
    c = lax.broadcasted_iota(jnp.int32, (PAGE_SIZE, PAGE_SIZE), 1)
    upper = _bf(jnp.where(r > c, 1.0, 0.0))
    qbd = qbd_ref[...]
    bias = bias_ref[...]

    def block(kt, vt, mask):
        z = _dot(qbd, kt) + bias
        av, rs = _sb_block(z, vt, upper, c_ref[...], mask)
        acc_ref[...] += av
        c_ref[...] += rs

    @pl.when(g == 0)
    def _():
        c_ref[...] = jnp.zeros_like(c_ref)
        acc_ref[...] = jnp.zeros_like(acc_ref)
        t = _div_pow2(lax.broadcasted_iota(jnp.int32, (rows, PAGE_SIZE), 0), SB_HEADS)
        s = lax.broadcasted_iota(jnp.int32, (rows, PAGE_SIZE), 1)
        block(kn_ref[...], vn_ref[...], s < t)

    for i in reversed(range(pp)):
        block(_bf(k_refs[i][...]), _bf(v_refs[i][...]), None)

    @pl.when(g == pl.num_programs(1) - 1)
    def _():
        rh = lax.broadcasted_iota(jnp.int32, (rows, SB_W), 0) & (SB_HEADS - 1)
        ch = _div_pow2(lax.broadcasted_iota(jnp.int32, (rows, SB_W), 1), HEAD_DIM)
        own = jnp.where(rh == ch, acc_ref[...], 0.0)
        o_ref[...] = _bf(jnp.sum(own.reshape(n_new, SB_HEADS, SB_W), axis=1))


def _sb_sample(page_table, bias_rows, qbd, knt, vnt, cache_kt, cache_vt, l, n_new):
    db, n_pages = page_table.shape
    pp = SAMPLE_PAGES_PER_STEP
    n_steps = n_pages // pp
    rows = qbd.shape[1]

    def page_spec(i):
        return pl.BlockSpec(
            (None, None, SB_W, PAGE_SIZE),
            lambda b, g, pt: (l, pt[b, (n_steps - 1 - g) * pp + i], 0, 0))

    grid_spec = pltpu.PrefetchScalarGridSpec(
        num_scalar_prefetch=1,
        grid=(db, n_steps),
        in_specs=[
            pl.BlockSpec((rows, 1), lambda b, g, pt: (0, 0)),
            pl.BlockSpec((None, rows, SB_W), lambda b, g, pt: (b, 0, 0)),
            pl.BlockSpec((None, SB_W, PAGE_SIZE), lambda b, g, pt: (b, 0, 0)),
            pl.BlockSpec((None, SB_W, PAGE_SIZE), lambda b, g, pt: (b, 0, 0)),
        ] + [page_spec(i) for i in range(pp)] * 2,
        out_specs=pl.BlockSpec((None, n_new, SB_W), lambda b, g, pt: (b, 0, 0)),
        scratch_shapes=[pltpu.VMEM((rows, 1), _f32), pltpu.VMEM((rows, SB_W), _f32)],
    )
    return pl.pallas_call(
        functools.partial(_sbs_kernel, n_new=n_new),
        grid_spec=grid_spec,
        out_shape=jax.ShapeDtypeStruct((db, n_new, SB_W), _bf16),
        compiler_params=_params(("parallel", "arbitrary"), VMEM_LIMIT),
        name="sb_sample",
    )(page_table, bias_rows, qbd, knt, vnt, *([cache_kt] * pp), *([cache_vt] * pp))


def _mixout_kernel(x_ref, gt_ref, npost_ref, og_ref, om_ref, os_ref, w_ref, o_ref):
    y = (_dot(og_ref[...], w_ref[0:GLA_W, :])
         + _dot(om_ref[...], w_ref[GLA_W:GLA_W + GM_W, :])
         + _dot(os_ref[...], w_ref[GLA_W + GM_W:, :]))
    o_ref[...] = x_ref[...] + gt_ref[...] * _rms(y, npost_ref[...])


def _mixout(x, cond, npost, o_gla, o_gm, o_sb, w_out, l, tm):
    m = x.shape[0]
    row = lambda i: (i, 0)
    return pl.pallas_call(
        _mixout_kernel,
        grid=(m // tm,),
        in_specs=[
            pl.BlockSpec((tm, D_MODEL), row),
            cond.spec(1, 2),
            pl.BlockSpec((1, D_MODEL), lambda i: (0, 0)),
            pl.BlockSpec((tm, GLA_W), row),
            pl.BlockSpec((tm, GM_W), row),
            pl.BlockSpec((tm, SB_W), row),
            _resident((None, D_MODEL, D_MODEL), lambda i: (l, 0, 0)),
        ],
        out_specs=pl.BlockSpec((tm, D_MODEL), row),
        out_shape=jax.ShapeDtypeStruct((m, D_MODEL), _f32),
        compiler_params=_params(("parallel",), VMEM_LIMIT),
        name="mixout",
    )(x, cond.arr, npost, o_gla, o_gm, o_sb, w_out)


def kernel(x_prompt, x_sample, c_prompt, c_sample, cache_sb_k, cache_sb_v, state_gla, page_table,
           cond_w, cond_b, norm_pre, norm_post, ffn_w_in, ffn_w_out, mix_w_in, mix_w_out,
           gla_w_gate2, gla_b_gate, gla_norm, gm_ln_w, gm_ln_b, gm_ws, gm_bs, sb_bias):
    depth = cond_w.shape[0]
    bsz, seq, _ = x_prompt.shape
    db, n_new, _ = x_sample.shape
    n_pool = cache_sb_k.shape[1]
    mp, ms = bsz * seq, db * n_new
    assert ms == GM_CHUNK and GM_CHUNK % n_new == 0 and seq % TM_PROMPT == 0

    w_in_b = _bf(ffn_w_in)
    w_out_b = _bf(ffn_w_out)
    mix_out_b = _bf(mix_w_out)
    lr0 = 4 * GLA_W
    gm0 = lr0 + GLA_LOWRANK
    sk0 = gm0 + 2 * GM_W + SB_W
    mix_nn = _bf(jnp.concatenate(
        [mix_w_in[:, :, :lr0], mix_w_in[:, :, gm0:sk0], mix_w_in[:, :, lr0:gm0],
         jnp.zeros((depth, D_MODEL, LR_PAD - GLA_LOWRANK), _f32)], axis=-1))
    mix_t = _bf(jnp.swapaxes(mix_w_in[:, :, sk0:], 1, 2))
    w2_pad = _bf(jnp.pad(gla_w_gate2, ((0, 0), (0, LR_PAD - GLA_LOWRANK), (0, 0))))
    to_t = lambda cch: jnp.transpose(cch, (0, 1, 3, 4, 2)).reshape(depth, n_pool, SB_W, PAGE_SIZE)
    cache_kt, cache_vt = to_t(cache_sb_k), to_t(cache_sb_v)
    head_cols = (jnp.arange(SB_W) // HEAD_DIM)[None, :] == jnp.arange(SB_HEADS)[:, None]

    m_all = _cond(jnp.concatenate([c_prompt, c_sample], axis=0), cond_w, cond_b)

    xp = x_prompt.reshape(mp, D_MODEL)
    xs = x_sample.reshape(ms, D_MODEL)
    s0_prompt = jnp.zeros((bsz, GLA_HEADS, HEAD_DIM, HEAD_DIM), _f32)
    gla_p, gla_s, kp, vp, ksl, vsl, gms = [], [], [], [], [], [], []
    vec = lambda a: a.reshape(1, -1)
    from_t = lambda t: jnp.transpose(t.reshape(t.shape[0], SB_HEADS, HEAD_DIM, -1), (0, 3, 1, 2))

    for l in range(depth):
        npre = [vec(norm_pre[l, i]) for i in range(3)]
        npost = [vec(norm_post[l, i]) for i in range(3)]
        bg, nw = vec(gla_b_gate[l]), vec(gla_norm[l])
        lnw, lnb = vec(gm_ln_w[l]), vec(gm_ln_b[l])

        cond = _Cond(m_all[l, :bsz], seq, TM_PROMPT)
        xp = _ffn(xp, cond, 0, npre[0], npost[0], w_in_b, w_out_b, l, 0, TM_PROMPT)
        bs_rows = jnp.repeat(gm_bs[l].T, HEAD_DIM, axis=1)
        gla, la, o_gm, _, qb, kt, vt, ktb, vtb = _mixin(
            xp, cond, npre[1], mix_nn, mix_t, l, w2_pad[l], bg, lnw, lnb, gm_ws[l], bs_rows,
            TM_PROMPT, GM_CHUNK, seq, SB_TQ)
        o_gla, s_gla = _gla(gla.reshape(bsz, seq, 4 * GLA_W), la.reshape(bsz, seq, GLA_W), nw,
                            s0_prompt, GLA_TL, GLA_CHUNK)
        o_sb = _sb_prompt(qb.reshape(bsz, seq, SB_W), ktb, vtb, sb_bias[l])
        xp = _mixout(xp, cond, npost[1], o_gla.reshape(mp, GLA_W), o_gm, o_sb.reshape(mp, SB_W),
                     mix_out_b, l, TM_PROMPT)
        xp = _ffn(xp, cond, 2, npre[2], npost[2], w_in_b, w_out_b, l, 1, TM_PROMPT)
        gla_p.append(s_gla)
        kp.append(from_t(kt))
        vp.append(from_t(vt))

        cond = _Cond(m_all[l, bsz:], n_new, ms)
        xs = _ffn(xs, cond, 0, npre[0], npost[0], w_in_b, w_out_b, l, 0, ms)
        reps = GM_CHUNK // n_new
        ws_s = jnp.tile(gm_ws[l][:, :n_new, :n_new], (1, reps, reps))
        bs_s = jnp.tile(bs_rows[:n_new], (reps, 1))
        gla, la, o_gm, vn, qb, kt, vt, ktb, vtb = _mixin(
            xs, cond, npre[1], mix_nn, mix_t, l, w2_pad[l], bg, lnw, lnb, ws_s, bs_s,
            ms, n_new, ms, ms)
        pad_rows = lambda a, n: jnp.pad(a.reshape(db, n_new, -1), ((0, 0), (0, n - n_new), (0, 0)))
        o_gla, s_gla = _gla(pad_rows(gla, SUBLANES), pad_rows(la, SUBLANES), nw, state_gla[l],
                            SUBLANES, SUBLANES)
        qbd = (qb.reshape(db, n_new, 1, SB_W) * head_cols.astype(_bf16)[None, None]
               ).reshape(db, n_new * SB_HEADS, SB_W)
        bias_rows = jnp.tile(sb_bias[l], n_new).reshape(n_new * SB_HEADS, 1)
        new_page = lambda t: jnp.pad(jnp.transpose(t.reshape(SB_W, db, n_new), (1, 0, 2)),
                                     ((0, 0), (0, 0), (0, PAGE_SIZE - n_new)))
        o_sb = _sb_sample(page_table, bias_rows, qbd, new_page(ktb), new_page(vtb),
                          cache_kt, cache_vt, l, n_new)
        xs = _mixout(xs, cond, npost[1], o_gla[:, :n_new].reshape(ms, GLA_W), o_gm,
                     o_sb.reshape(ms, SB_W), mix_out_b, l, ms)
        xs = _ffn(xs, cond, 2, npre[2], npost[2], w_in_b, w_out_b, l, 1, ms)
        gla_s.append(s_gla)
        ksl.append(kt.reshape(SB_W, ms).T.reshape(db, n_new, SB_HEADS, HEAD_DIM))
        vsl.append(vt.reshape(SB_W, ms).T.reshape(db, n_new, SB_HEADS, HEAD_DIM))
        gms.append(vn.reshape(db, n_new, GM_W))

    return (xp.reshape(bsz, seq, D_MODEL), xs.reshape(db, n_new, D_MODEL),
            jnp.stack(gla_p), jnp.stack(gla_s), jnp.stack(kp), jnp.stack(vp),
            jnp.stack(ksl), jnp.stack(vsl), jnp.stack(gms))
```

```python
import functools

import jax
import jax.numpy as jnp
from jax import lax
from jax.experimental import pallas as pl
from jax.experimental.pallas import tpu as pltpu

D_MODEL = 1024
HEAD_DIM = 64
GLA_HEADS = 4
GM_HEADS = 4
SB_HEADS = 8
GLA_W = GLA_HEADS * HEAD_DIM
GM_W = GM_HEADS * HEAD_DIM
SB_W = SB_HEADS * HEAD_DIM
GLA_LOWRANK = 16
GLA_TAU = 16.0
GLA_CHUNK = 64
GM_CHUNK = 128
PAGE_SIZE = 128
D_FF = 2816
N_COND = 9
MACARON_W = 0.5
EPS = 1e-6
LOG2E = 1.4426950408889634

LANES = 128
SUBLANES = 8
LR_PAD = LANES
COL_GLA = 0
COL_GM = 4 * GLA_W
COL_SBQ = COL_GM + 2 * GM_W
COL_LR = COL_SBQ + SB_W
MIX_COLS = COL_LR + LR_PAD

FFN_CHUNK = 256
TM_PROMPT = 512
SB_TQ = 256
GLA_TL = 256
SAMPLE_PAGES_PER_STEP = 16
VMEM_LIMIT = 48 * 1024 * 1024

_f32 = jnp.float32
_bf16 = jnp.bfloat16


def _bf(x):
    return x.astype(_bf16)


def _dot(a, b):
    return jnp.dot(a, b, preferred_element_type=_f32)


def _dot_nt(a, b):
    return lax.dot_general(a, b, (((1,), (1,)), ((), ())), preferred_element_type=_f32)


def _dot_tn(a, b):
    return lax.dot_general(a, b, (((0,), (0,)), ((), ())), preferred_element_type=_f32)


def _split_bf16(x, passes):
    pieces = []
    rem = x
    for _ in range(passes):
        piece = _bf(rem)
        pieces.append(piece)
        rem = rem - piece.astype(_f32)
    return pieces


def _dot_split(x, m, passes):
    return sum(_dot(p, m) for p in _split_bf16(x, passes))


def _cumsum_rows(x, tri):
    return sum(_dot(tri, p) for p in _split_bf16(x, 3))


def _rms(x, w):
    return x * lax.rsqrt(jnp.mean(x * x, axis=-1, keepdims=True) + EPS) * w


def _sigmoid(x):
    return 1.0 / (1.0 + jnp.exp(-x))


def _log1p_exp_neg_abs(z):
    return jnp.log(1.0 + jnp.exp(-jnp.abs(z)))


def _softplus(z):
    return jnp.maximum(z, 0.0) + jnp.log(1.0 + jnp.exp2(jnp.abs(z) * (-LOG2E)))


def _div_pow2(x, d):
    shift = d.bit_length() - 1
    assert d == 1 << shift
    return lax.shift_right_logical(x, shift)


def _params(sem, vmem=None):
    return pltpu.CompilerParams(dimension_semantics=sem, vmem_limit_bytes=vmem)


def _resident(shape, index_map):
    return pl.BlockSpec(shape, index_map, pipeline_mode=pl.Buffered(1))


def _cond_kernel(c_ref, w_ref, b_ref, o_ref):
    c = c_ref[...]
    s = c * _sigmoid(c)
    o_ref[...] = _dot(_bf(s), _bf(w_ref[...])) + b_ref[...]


def _cond(c_all, cond_w, cond_b):
    depth = cond_w.shape[0]
    n = c_all.shape[0]
    tn = D_MODEL
    return pl.pallas_call(
        _cond_kernel,
        grid=(depth, N_COND * D_MODEL // tn),
        in_specs=[
            pl.BlockSpec((n, D_MODEL), lambda l, j: (0, 0)),
            pl.BlockSpec((None, D_MODEL, tn), lambda l, j: (l, 0, j)),
            pl.BlockSpec((None, 1, tn), lambda l, j: (l, 0, j)),
        ],
        out_specs=pl.BlockSpec((None, n, tn), lambda l, j: (l, 0, j)),
        out_shape=jax.ShapeDtypeStruct((depth, n, N_COND * D_MODEL), _f32),
        compiler_params=_params(("parallel", "parallel")),
        name="cond",
    )(c_all, cond_w, cond_b.reshape(depth, 1, N_COND * D_MODEL))


class _Cond:
    def __init__(self, m, seq_rows, tm):
        if seq_rows % tm == 0:
            self.arr = m.reshape(m.shape[0], 1, N_COND * D_MODEL)
            per = seq_rows // tm
            self._spec = lambda j: pl.BlockSpec((None, 1, D_MODEL), lambda i: (i // per, 0, j))
        else:
            assert tm % seq_rows == 0
            self.arr = jnp.repeat(m, seq_rows, axis=0)
            self._spec = lambda j: pl.BlockSpec((tm, D_MODEL), lambda i: (i, j))

    def spec(self, sub, kind):
        return self._spec(3 * sub + kind)


def _ffn_kernel(x_ref, sh_ref, sc_ref, gt_ref, npre_ref, npost_ref, win_ref, wout_ref, o_ref,
                a_ref, acc_ref):
    x = x_ref[...]
    a = _rms(x, npre_ref[...]) * (1.0 + sc_ref[...]) + sh_ref[...]
    a_ref[...] = _bf(a)
    for c in range(D_FF // FFN_CHUNK):
        lo = c * FFN_CHUNK
        ab = a_ref[...]
        g = _dot(ab, win_ref[:, lo:lo + FFN_CHUNK])
        u = _dot(ab, win_ref[:, D_FF + lo:D_FF + lo + FFN_CHUNK])
        act = _bf(g * _sigmoid(g) * u)
        y = _dot(act, wout_ref[lo:lo + FFN_CHUNK, :])
        if c == 0:
            acc_ref[...] = y
        else:
            acc_ref[...] += y
    o_ref[...] = x + MACARON_W * gt_ref[...] * _rms(acc_ref[...], npost_ref[...])


def _ffn(x, cond, sub, npre, npost, w_in, w_out, l, idx, tm):
    m = x.shape[0]
    row = lambda i: (i, 0)
    vec = pl.BlockSpec((1, D_MODEL), lambda i: (0, 0))
    return pl.pallas_call(
        _ffn_kernel,
        grid=(m // tm,),
        in_specs=[
            pl.BlockSpec((tm, D_MODEL), row),
            cond.spec(sub, 0), cond.spec(sub, 1), cond.spec(sub, 2),
            vec, vec,
            _resident((None, None, D_MODEL, 2 * D_FF), lambda i: (l, idx, 0, 0)),
            _resident((None, None, D_FF, D_MODEL), lambda i: (l, idx, 0, 0)),
        ],
        out_specs=pl.BlockSpec((tm, D_MODEL), row),
        out_shape=jax.ShapeDtypeStruct((m, D_MODEL), _f32),
        scratch_shapes=[pltpu.VMEM((tm, D_MODEL), _bf16), pltpu.VMEM((tm, D_MODEL), _f32)],
        compiler_params=_params(("parallel",), VMEM_LIMIT),
        name="ffn",
    )(x, cond.arr, cond.arr, cond.arr, npre, npost, w_in, w_out)


def _mixin_kernel(x_ref, sh_ref, sc_ref, npre_ref, w_ref, wt_ref, w2_ref, bg_ref, lnw_ref, lnb_ref,
                  ws_ref, bs_ref,
                  gla_ref, la_ref, ogm_ref, vn_ref, qb_ref, kt_ref, vt_ref, ktb_ref, vtb_ref,
                  *, period):
    tm = x_ref.shape[0]
    tk = ktb_ref.shape[-1]
    x = x_ref[...]
    a = _bf(_rms(x, npre_ref[...]) * (1.0 + sc_ref[...]) + sh_ref[...])

    gla_ref[...] = _dot(a, w_ref[:, COL_GLA:COL_GLA + 4 * GLA_W])
    lr = _dot(a, w_ref[:, COL_LR:COL_LR + LR_PAD])
    zl = _dot(_bf(lr), w2_ref[...]) + bg_ref[...]
    la_ref[...] = (jnp.minimum(zl, 0.0) - _log1p_exp_neg_abs(zl)) * (1.0 / GLA_TAU)

    q = _dot(a, w_ref[:, COL_SBQ:COL_SBQ + SB_W])
    qb_ref[...] = _bf(q * (HEAD_DIM ** -0.5))
    kt = _dot_nt(wt_ref[0:SB_W, :], a)
    vt = _dot_nt(wt_ref[SB_W:2 * SB_W, :], a)
    kt_ref[...] = kt
    vt_ref[...] = vt
    for j in range(tm // tk):
        ktb_ref[j] = _bf(kt[:, j * tk:(j + 1) * tk])
        vtb_ref[j] = _bf(vt[:, j * tk:(j + 1) * tk])

    mu = _dot(a, w_ref[:, COL_GM:COL_GM + GM_W])
    mv = _dot(a, w_ref[:, COL_GM + GM_W:COL_GM + 2 * GM_W])
    mean = jnp.mean(mv, axis=-1, keepdims=True)
    d = mv - mean
    var = jnp.mean(d * d, axis=-1, keepdims=True)
    vn = d * lax.rsqrt(var + EPS) * lnw_ref[...] + lnb_ref[...]
    vn_ref[...] = vn
    vnb = _bf(vn)
    r = lax.broadcasted_iota(jnp.int32, (GM_CHUNK, GM_CHUNK), 0)
    c = lax.broadcasted_iota(jnp.int32, (GM_CHUNK, GM_CHUNK), 1)
    keep = (c <= r) & (_div_pow2(r, period) == _div_pow2(c, period))
    wm = [_bf(jnp.where(keep, ws_ref[h], 0.0)) for h in range(GM_HEADS)]
    col_head = _div_pow2(lax.broadcasted_iota(jnp.int32, (GM_CHUNK, GM_W), 1), HEAD_DIM)
    for ci in range(tm // GM_CHUNK):
        rows = slice(ci * GM_CHUNK, (ci + 1) * GM_CHUNK)
        vc = vnb[rows, :]
        s = bs_ref[...]
        for h in range(GM_HEADS):
            s = s + jnp.where(col_head == h, _dot(wm[h], vc), 0.0)
        ogm_ref[rows, :] = _bf(mu[rows, :] * s)


def _mixin(x, cond, npre, w_nn, w_t, l, w2p, bg, lnw, lnb, ws, bs, tm, period, seq, tk):
    m = x.shape[0]
    n_seq, per = m // seq, seq // tm
    row = lambda i: (i, 0)
    const2 = lambda i: (0, 0)
    rows_out = [(4 * GLA_W, _f32), (GLA_W, _f32), (GM_W, _bf16), (GM_W, _f32), (SB_W, _bf16)]
    t_spec = pl.BlockSpec((None, SB_W, tm), lambda i: (i // per, 0, i % per))
    tb_spec = pl.BlockSpec((None, tm // tk, SB_W, tk), lambda i: (i // per, i % per, 0, 0))
    t_shape = jax.ShapeDtypeStruct((n_seq, SB_W, seq), _f32)
    tb_shape = jax.ShapeDtypeStruct((n_seq, seq // tk, SB_W, tk), _bf16)
    return pl.pallas_call(
        functools.partial(_mixin_kernel, period=period),
        grid=(m // tm,),
        in_specs=[
            pl.BlockSpec((tm, D_MODEL), row),
            cond.spec(1, 0), cond.spec(1, 1),
            pl.BlockSpec((1, D_MODEL), const2),
            _resident((None, D_MODEL, MIX_COLS), lambda i: (l, 0, 0)),
            _resident((None, 2 * SB_W, D_MODEL), lambda i: (l, 0, 0)),
            pl.BlockSpec((LR_PAD, GLA_W), const2),
            pl.BlockSpec((1, GLA_W), const2),
            pl.BlockSpec((1, GM_W), const2),
            pl.BlockSpec((1, GM_W), const2),
            pl.BlockSpec((GM_HEADS, GM_CHUNK, GM_CHUNK), lambda i: (0, 0, 0)),
            pl.BlockSpec((GM_CHUNK, GM_W), const2),
        ],
        out_specs=[pl.BlockSpec((tm, w), row) for w, _ in rows_out] + [t_spec, t_spec, tb_spec, tb_spec],
        out_shape=[jax.ShapeDtypeStruct((m, w), dt) for w, dt in rows_out]
        + [t_shape, t_shape, tb_shape, tb_shape],
        compiler_params=_params(("parallel",), VMEM_LIMIT),
        name="mixin",
    )(x, cond.arr, cond.arr, npre, w_nn, w_t, w2p, bg, lnw, lnb, ws, bs)


def _gla_kernel(gla_ref, la_ref, nw_ref, s0_ref, o_ref, s_ref, st_ref, ob_ref, *, chunk):
    tl = gla_ref.shape[0]
    j = pl.program_id(1)

    @pl.when(j == 0)
    def _():
        for h in range(GLA_HEADS):
            st_ref[h] = s0_ref[h].T

    ri = lax.broadcasted_iota(jnp.int32, (chunk, chunk), 0)
    ci = lax.broadcasted_iota(jnp.int32, (chunk, chunk), 1)
    causal = ci <= ri
    tri = _bf(jnp.where(causal, 1.0, 0.0))
    gi = _div_pow2(lax.broadcasted_iota(jnp.int32, (GLA_W, GLA_W), 0), HEAD_DIM)
    gj = _div_pow2(lax.broadcasted_iota(jnp.int32, (GLA_W, GLA_W), 1), HEAD_DIM)
    group = _bf(jnp.where(gi == gj, 1.0, 0.0))

    for c in range(tl // chunk):
        rows = slice(c * chunk, (c + 1) * chunk)
        q = gla_ref[rows, 0:GLA_W] * (HEAD_DIM ** -0.5)
        k = gla_ref[rows, GLA_W:2 * GLA_W]
        v = gla_ref[rows, 2 * GLA_W:3 * GLA_W]
        g = gla_ref[rows, 3 * GLA_W:4 * GLA_W]
        b = _cumsum_rows(la_ref[rows, :], tri)
        b_last = b[chunk - 1:chunk, :]
        q_t = q * jnp.exp(b)
        k_t = k * jnp.exp(-b)
        k_end = k * jnp.exp(b_last - b)
        decay = jnp.exp(b_last)
        for h in range(GLA_HEADS):
            hs = slice(h * HEAD_DIM, (h + 1) * HEAD_DIM)
            qh = _bf(q_t[:, hs])
            vh = _bf(v[:, hs])
            att = jnp.where(causal, _dot_nt(qh, _bf(k_t[:, hs])), 0.0)
            st = st_ref[h]
            ob_ref[rows, hs] = _dot_nt(qh, _bf(st)) + _dot(_bf(att), vh)
            st_ref[h] = st * decay[:, hs] + _dot_tn(vh, _bf(k_end[:, hs]))
        o = ob_ref[rows, :]
        ms = _dot_split(o * o, group, 2) * (1.0 / HEAD_DIM)
        y = o * lax.rsqrt(ms + EPS) * nw_ref[...]
        o_ref[rows, :] = _bf(y * (g * _sigmoid(g)))

    @pl.when(j == pl.num_programs(1) - 1)
    def _():
        for h in range(GLA_HEADS):
            s_ref[h] = st_ref[h].T


def _gla(gla, la, nw, s0, tl, chunk):
    bsz, length, _ = gla.shape
    state_spec = pl.BlockSpec((None, GLA_HEADS, HEAD_DIM, HEAD_DIM), lambda b, j: (b, 0, 0, 0))
    return pl.pallas_call(
        functools.partial(_gla_kernel, chunk=chunk),
        grid=(bsz, length // tl),
        in_specs=[
            pl.BlockSpec((None, tl, 4 * GLA_W), lambda b, j: (b, j, 0)),
            pl.BlockSpec((None, tl, GLA_W), lambda b, j: (b, j, 0)),
            pl.BlockSpec((1, GLA_W), lambda b, j: (0, 0)),
            state_spec,
        ],
        out_specs=[pl.BlockSpec((None, tl, GLA_W), lambda b, j: (b, j, 0)), state_spec],
        out_shape=[
            jax.ShapeDtypeStruct((bsz, length, GLA_W), _bf16),
            jax.ShapeDtypeStruct((bsz, GLA_HEADS, HEAD_DIM, HEAD_DIM), _f32),
        ],
        scratch_shapes=[
            pltpu.VMEM((GLA_HEADS, HEAD_DIM, HEAD_DIM), _f32),
            pltpu.VMEM((tl, GLA_W), _f32),
        ],
        compiler_params=_params(("parallel", "arbitrary")),
        name="gla",
    )(gla, la, nw, s0)


def _sb_table(tk, with_total):
    cols = tk + LANES if with_total else tk
    r = lax.broadcasted_iota(jnp.int32, (2 * tk, cols), 0) & (tk - 1)
    c = lax.broadcasted_iota(jnp.int32, (2 * tk, cols), 1)
    return _bf(jnp.where((r > c) | (c >= tk), 1.0, 0.0))


def _sb_weights(z, table, c_run, mask):
    tk = z.shape[1]
    sp = _softplus(z)
    if mask is not None:
        sp = jnp.where(mask, sp, 0.0)
    sums = _dot(jnp.concatenate(_split_bf16(sp, 2), axis=1), table)
    a = jnp.exp((z - sp) - (sums[:, :tk] + c_run))
    if mask is not None:
        a = jnp.where(mask, a, 0.0)
    return _bf(a), sp, sums


def _sbp_kernel(bias_ref, q_ref, k_ref, v_ref, o_ref,
                table_ref, qh_ref, c_ref, acc_ref, zs_ref, sp_ref, a_ref):
    tq = q_ref.shape[0]
    assert k_ref.shape[-1] == tq
    i = pl.program_id(1)
    table_ref[...] = _sb_table(tq, with_total=False)
    heads = [slice(h * HEAD_DIM, (h + 1) * HEAD_DIM) for h in range(SB_HEADS)]
    for h, hs in enumerate(heads):
        qh_ref[h] = q_ref[:, hs]

    def visit(j, mask, first):
        for h, hs in enumerate(heads):
            z = _dot(qh_ref[h], k_ref[j, hs, :]) + bias_ref[h]
            sp = _softplus(z)
            if mask is not None:
                sp = jnp.where(mask, sp, 0.0)
            zs_ref[h] = z - sp
            sp_ref[h] = jnp.concatenate(_split_bf16(sp, 2), axis=1)
        for h in range(SB_HEADS):
            nearer = _dot(sp_ref[h], table_ref[...])
            c_run = jnp.zeros((tq, LANES), _f32) if first else c_ref[h]
            a = jnp.exp(zs_ref[h] - (nearer + jnp.concatenate([c_run] * (tq // LANES), axis=1)))
            if mask is not None:
                a = jnp.where(mask, a, 0.0)
            a_ref[h] = _bf(a)
            first_key = sp_ref[h, :, 0:1].astype(_f32) + sp_ref[h, :, tq:tq + 1].astype(_f32)
            block_sum = jnp.broadcast_to(nearer[:, 0:1] + first_key, (tq, LANES))
            c_ref[h] = block_sum if first else c_run + block_sum
        for h, hs in enumerate(heads):
            av = _dot_nt(a_ref[h], v_ref[j, hs, :])
            acc_ref[h] = av if first else acc_ref[h] + av

    r = lax.broadcasted_iota(jnp.int32, (tq, tq), 0)
    c = lax.broadcasted_iota(jnp.int32, (tq, tq), 1)
    visit(i, c < r, True)

    def body(jj, carry):
        visit(i - 1 - jj, None, False)
        return carry

    lax.fori_loop(0, i, body, 0)
    for h, hs in enumerate(heads):
        o_ref[:, hs] = _bf(acc_ref[h])


def _sb_prompt(qb, ktb, vtb, bias):
    bsz, length, _ = qb.shape
    nblk, _, tq = ktb.shape[1:]
    kv_spec = pl.BlockSpec((None, nblk, SB_W, tq), lambda b, i: (b, 0, 0, 0))
    return pl.pallas_call(
        _sbp_kernel,
        grid=(bsz, nblk),
        in_specs=[
            pl.BlockSpec(memory_space=pltpu.SMEM),
            pl.BlockSpec((None, tq, SB_W), lambda b, i: (b, i, 0)),
            kv_spec, kv_spec,
        ],
        out_specs=pl.BlockSpec((None, tq, SB_W), lambda b, i: (b, i, 0)),
        out_shape=jax.ShapeDtypeStruct((bsz, length, SB_W), _bf16),
        scratch_shapes=[
            pltpu.VMEM((2 * tq, tq), _bf16),
            pltpu.VMEM((SB_HEADS, tq, HEAD_DIM), _bf16),
            pltpu.VMEM((SB_HEADS, tq, LANES), _f32),
            pltpu.VMEM((SB_HEADS, tq, HEAD_DIM), _f32),
            pltpu.VMEM((SB_HEADS, tq, tq), _f32),
            pltpu.VMEM((SB_HEADS, tq, 2 * tq), _bf16),
            pltpu.VMEM((SB_HEADS, tq, tq), _bf16),
        ],
        compiler_params=_params(("parallel", "arbitrary"), VMEM_LIMIT),
        name="sb_prompt",
    )(bias, qb, ktb, vtb)


def _sbs_kernel(pt_ref, bias_ref, qbd_ref, kn_ref, vn_ref, *refs, n_new):
    pp = SAMPLE_PAGES_PER_STEP
    k_refs, v_refs = refs[:pp], refs[pp:2 * pp]
    o_ref, c_ref, acc_ref, z_ref, later_ref = refs[2 * pp:]
    g = pl.program_id(1)
    rows = qbd_ref.shape[0]
    n = pp * rows
    table = _sb_table(PAGE_SIZE, with_total=True)
    qbd = qbd_ref[...]
    bias = bias_ref[...]

    @pl.when(g == 0)
    def _():
        ri = lax.broadcasted_iota(jnp.int32, (n, n), 0)
        ci = lax.broadcasted_iota(jnp.int32, (n, n), 1)
        same_row = (ri & (rows - 1)) == (ci & (rows - 1))
        later_ref[...] = _bf(jnp.where(same_row & (_div_pow2(ci, rows) > _div_pow2(ri, rows)), 1.0, 0.0))
        t = _div_pow2(lax.broadcasted_iota(jnp.int32, (rows, PAGE_SIZE), 0), SB_HEADS)
        s = lax.broadcasted_iota(jnp.int32, (rows, PAGE_SIZE), 1)
        a, _, sums = _sb_weights(_dot(qbd, kn_ref[...]) + bias, table,
                                 jnp.zeros((rows, PAGE_SIZE), _f32), s < t)
        acc_ref[...] = _dot_nt(a, vn_ref[...])
        c_ref[...] = sums[:, PAGE_SIZE:]

    for p in range(pp):
        z_ref[p * rows:(p + 1) * rows, :] = _dot(qbd, _bf(k_refs[p][...])) + bias
    z = z_ref[...]
    sp = _softplus(z)
    sums = _dot(jnp.concatenate(_split_bf16(sp, 2), axis=1), table)
    page_sum = sums[:, PAGE_SIZE:]
    c_in = c_ref[...]
    nearer = (sum(_dot(later_ref[...], piece) for piece in _split_bf16(page_sum, 2))
              + jnp.concatenate([c_in] * pp, axis=0))
    a = _bf(jnp.exp((z - sp) - (sums[:, :PAGE_SIZE] + nearer)))
    acc = acc_ref[...]
    c_out = c_in
    for p in range(pp):
        acc = acc + _dot_nt(a[p * rows:(p + 1) * rows, :], _bf(v_refs[p][...]))
        c_out = c_out + page_sum[p * rows:(p + 1) * rows, :]
    acc_ref[...] = acc
    c_ref[...] = c_out

    @pl.when(g == pl.num_programs(1) - 1)
    def _():
        rh = lax.broadcasted_iota(jnp.int32, (rows, SB_W), 0) & (SB_HEADS - 1)
        ch = _div_pow2(lax.broadcasted_iota(jnp.int32, (rows, SB_W), 1), HEAD_DIM)
        own = jnp.where(rh == ch, acc_ref[...], 0.0)
        o_ref[...] = _bf(jnp.sum(own.reshape(n_new, SB_HEADS, SB_W), axis=1))


def _sb_sample(page_table, bias_rows, qbd, knt, vnt, cache_kt, cache_vt, l, n_new):
    db, n_pages = page_table.shape
    pp = SAMPLE_PAGES_PER_STEP
    n_steps = n_pages // pp
    rows = qbd.shape[1]

    def page_spec(i):
        return pl.BlockSpec(
            (None, None, SB_W, PAGE_SIZE),
            lambda b, g, pt: (l, pt[b, (n_steps - 1 - g) * pp + i], 0, 0))

    grid_spec = pltpu.PrefetchScalarGridSpec(
        num_scalar_prefetch=1,
        grid=(db, n_steps),
        in_specs=[
            pl.BlockSpec((rows, LANES), lambda b, g, pt: (0, 0)),
            pl.BlockSpec((None, rows, SB_W), lambda b, g, pt: (b, 0, 0)),
            pl.BlockSpec((None, SB_W, PAGE_SIZE), lambda b, g, pt: (b, 0, 0)),
            pl.BlockSpec((None, SB_W, PAGE_SIZE), lambda b, g, pt: (b, 0, 0)),
        ] + [page_spec(i) for i in range(pp)] * 2,
        out_specs=pl.BlockSpec((None, n_new, SB_W), lambda b, g, pt: (b, 0, 0)),
        scratch_shapes=[
            pltpu.VMEM((rows, LANES), _f32),
            pltpu.VMEM((rows, SB_W), _f32),
            pltpu.VMEM((pp * rows, PAGE_SIZE), _f32),
            pltpu.VMEM((pp * rows, pp * rows), _bf16),
        ],
    )
    return pl.pallas_call(
        functools.partial(_sbs_kernel, n_new=n_new),
        grid_spec=grid_spec,
        out_shape=jax.ShapeDtypeStruct((db, n_new, SB_W), _bf16),
        compiler_params=_params(("parallel", "arbitrary"), VMEM_LIMIT),
        name="sb_sample",
    )(page_table, bias_rows, qbd, knt, vnt, *([cache_kt] * pp), *([cache_vt] * pp))


def _mixout_kernel(x_ref, gt_ref, npost_ref, og_ref, om_ref, os_ref, w_ref, o_ref):
    y = (_dot(og_ref[...], w_ref[0:GLA_W, :])
         + _dot(om_ref[...], w_ref[GLA_W:GLA_W + GM_W, :])
         + _dot(os_ref[...], w_ref[GLA_W + GM_W:, :]))
    o_ref[...] = x_ref[...] + gt_ref[...] * _rms(y, npost_ref[...])


def _mixout(x, cond, npost, o_gla, o_gm, o_sb, w_out, l, tm):
    m = x.shape[0]
    row = lambda i: (i, 0)
    return pl.pallas_call(
        _mixout_kernel,
        grid=(m // tm,),
        in_specs=[
            pl.BlockSpec((tm, D_MODEL), row),
            cond.spec(1, 2),
            pl.BlockSpec((1, D_MODEL), lambda i: (0, 0)),
            pl.BlockSpec((tm, GLA_W), row),
            pl.BlockSpec((tm, GM_W), row),
            pl.BlockSpec((tm, SB_W), row),
            _resident((None, D_MODEL, D_MODEL), lambda i: (l, 0, 0)),
        ],
        out_specs=pl.BlockSpec((tm, D_MODEL), row),
        out_shape=jax.ShapeDtypeStruct((m, D_MODEL), _f32),
        compiler_params=_params(("parallel",), VMEM_LIMIT),
        name="mixout",
    )(x, cond.arr, npost, o_gla, o_gm, o_sb, w_out)


def kernel(x_prompt, x_sample, c_prompt, c_sample, cache_sb_k, cache_sb_v, state_gla, page_table,
           cond_w, cond_b, norm_pre, norm_post, ffn_w_in, ffn_w_out, mix_w_in, mix_w_out,
           gla_w_gate2, gla_b_gate, gla_norm, gm_ln_w, gm_ln_b, gm_ws, gm_bs, sb_bias):
    depth = cond_w.shape[0]
    bsz, seq, _ = x_prompt.shape
    db, n_new, _ = x_sample.shape
    n_pool = cache_sb_k.shape[1]
    mp, ms = bsz * seq, db * n_new
    assert ms == GM_CHUNK and GM_CHUNK % n_new == 0 and seq % TM_PROMPT == 0

    w_in_b = _bf(ffn_w_in)
    w_out_b = _bf(ffn_w_out)
    mix_out_b = _bf(mix_w_out)
    lr0 = 4 * GLA_W
    gm0 = lr0 + GLA_LOWRANK
    sk0 = gm0 + 2 * GM_W + SB_W
    mix_nn = _bf(jnp.concatenate(
        [mix_w_in[:, :, :lr0], mix_w_in[:, :, gm0:sk0], mix_w_in[:, :, lr0:gm0],
         jnp.zeros((depth, D_MODEL, LR_PAD - GLA_LOWRANK), _f32)], axis=-1))
    mix_t = _bf(jnp.swapaxes(mix_w_in[:, :, sk0:], 1, 2))
    w2_pad = _bf(jnp.pad(gla_w_gate2, ((0, 0), (0, LR_PAD - GLA_LOWRANK), (0, 0))))
    to_t = lambda cch: jnp.transpose(cch, (0, 1, 3, 4, 2)).reshape(depth, n_pool, SB_W, PAGE_SIZE)
    cache_kt, cache_vt = to_t(cache_sb_k), to_t(cache_sb_v)
    head_cols = (jnp.arange(SB_W) // HEAD_DIM)[None, :] == jnp.arange(SB_HEADS)[:, None]

    m_all = _cond(jnp.concatenate([c_prompt, c_sample], axis=0), cond_w, cond_b)

    xp = x_prompt.reshape(mp, D_MODEL)
    xs = x_sample.reshape(ms, D_MODEL)
    s0_prompt = jnp.zeros((bsz, GLA_HEADS, HEAD_DIM, HEAD_DIM), _f32)
    gla_p, gla_s, kp, vp, ksl, vsl, gms = [], [], [], [], [], [], []
    vec = lambda a: a.reshape(1, -1)
    from_t = lambda t: jnp.transpose(t.reshape(t.shape[0], SB_HEADS, HEAD_DIM, -1), (0, 3, 1, 2))

    for l in range(depth):
        npre = [vec(norm_pre[l, i]) for i in range(3)]
        npost = [vec(norm_post[l, i]) for i in range(3)]
        bg, nw = vec(gla_b_gate[l]), vec(gla_norm[l])
        lnw, lnb = vec(gm_ln_w[l]), vec(gm_ln_b[l])

        cond = _Cond(m_all[l, :bsz], seq, TM_PROMPT)
        xp = _ffn(xp, cond, 0, npre[0], npost[0], w_in_b, w_out_b, l, 0, TM_PROMPT)
        bs_rows = jnp.repeat(gm_bs[l].T, HEAD_DIM, axis=1)
        gla, la, o_gm, _, qb, kt, vt, ktb, vtb = _mixin(
            xp, cond, npre[1], mix_nn, mix_t, l, w2_pad[l], bg, lnw, lnb, gm_ws[l], bs_rows,
            TM_PROMPT, GM_CHUNK, seq, SB_TQ)
        o_gla, s_gla = _gla(gla.reshape(bsz, seq, 4 * GLA_W), la.reshape(bsz, seq, GLA_W), nw,
                            s0_prompt, GLA_TL, GLA_CHUNK)
        o_sb = _sb_prompt(qb.reshape(bsz, seq, SB_W), ktb, vtb, sb_bias[l])
        xp = _mixout(xp, cond, npost[1], o_gla.reshape(mp, GLA_W), o_gm, o_sb.reshape(mp, SB_W),
                     mix_out_b, l, TM_PROMPT)
        xp = _ffn(xp, cond, 2, npre[2], npost[2], w_in_b, w_out_b, l, 1, TM_PROMPT)
        gla_p.append(s_gla)
        kp.append(from_t(kt))
        vp.append(from_t(vt))

        cond = _Cond(m_all[l, bsz:], n_new, ms)
        xs = _ffn(xs, cond, 0, npre[0], npost[0], w_in_b, w_out_b, l, 0, ms)
        reps = GM_CHUNK // n_new
        ws_s = jnp.tile(gm_ws[l][:, :n_new, :n_new], (1, reps, reps))
        bs_s = jnp.tile(bs_rows[:n_new], (reps, 1))
        gla, la, o_gm, vn, qb, kt, vt, ktb, vtb = _mixin(
            xs, cond, npre[1], mix_nn, mix_t, l, w2_pad[l], bg, lnw, lnb, ws_s, bs_s,
            ms, n_new, ms, ms)
        pad_rows = lambda a, n: jnp.pad(a.reshape(db, n_new, -1), ((0, 0), (0, n - n_new), (0, 0)))
        o_gla, s_gla = _gla(pad_rows(gla, SUBLANES), pad_rows(la, SUBLANES), nw, state_gla[l],
                            SUBLANES, SUBLANES)
        qbd = (qb.reshape(db, n_new, 1, SB_W) * head_cols.astype(_bf16)[None, None]
               ).reshape(db, n_new * SB_HEADS, SB_W)
        bias_rows = jnp.broadcast_to(jnp.tile(sb_bias[l], n_new)[:, None], (n_new * SB_HEADS, LANES))
        new_page = lambda t: jnp.pad(jnp.transpose(t.reshape(SB_W, db, n_new), (1, 0, 2)),
                                     ((0, 0), (0, 0), (0, PAGE_SIZE - n_new)))
        o_sb = _sb_sample(page_table, bias_rows, qbd, new_page(ktb), new_page(vtb),
                          cache_kt, cache_vt, l, n_new)
        xs = _mixout(xs, cond, npost[1], o_gla[:, :n_new].reshape(ms, GLA_W), o_gm,
                     o_sb.reshape(ms, SB_W), mix_out_b, l, ms)
        xs = _ffn(xs, cond, 2, npre[2], npost[2], w_in_b, w_out_b, l, 1, ms)
        gla_s.append(s_gla)
        ksl.append(kt.reshape(SB_W, ms).T.reshape(db, n_new, SB_HEADS, HEAD_DIM))
        vsl.append(vt.reshape(SB_W, ms).T.reshape(db, n_new, SB_HEADS, HEAD_DIM))
        gms.append(vn.reshape(db, n_new, GM_W))

    return (xp.reshape(bsz, seq, D_MODEL), xs.reshape(db, n_new, D_MODEL),
            jnp.stack(gla_p), jnp.stack(gla_s), jnp.stack(kp), jnp.stack(vp),
            jnp.stack(ksl), jnp.stack(vsl), jnp.stack(gms))
```

```python
import functools

import jax
import jax.numpy as jnp
from jax import lax
from jax.experimental import pallas as pl
from jax.experimental.pallas import tpu as pltpu

D_MODEL = 1024
HEAD_DIM = 64
GLA_HEADS = 4
GM_HEADS = 4
SB_HEADS = 8
GLA_W = GLA_HEADS * HEAD_DIM
GM_W = GM_HEADS * HEAD_DIM
SB_W = SB_HEADS * HEAD_DIM
GLA_LOWRANK = 16
GLA_TAU = 16.0
GLA_CHUNK = 64
GM_CHUNK = 128
PAGE_SIZE = 128
D_FF = 2816
N_COND = 9
MACARON_W = 0.5
EPS = 1e-6
LOG2E = 1.4426950408889634

LANES = 128
SUBLANES = 8
LR_PAD = LANES
COL_GLA = 0
COL_GM = 4 * GLA_W
COL_SBQ = COL_GM + 2 * GM_W
COL_LR = COL_SBQ + SB_W
MIX_COLS = COL_LR + LR_PAD

FFN_CHUNK = 256
TM_PROMPT = 512
SB_TQ = 256
GLA_TL = 512
GLA_SAMPLE_SEQS = 4
SAMPLE_PAGES_PER_STEP = 16
SAMPLE_PAGES_PER_GROUP = 4
VMEM_LIMIT = 48 * 1024 * 1024

_f32 = jnp.float32
_bf16 = jnp.bfloat16


def _bf(x):
    return x.astype(_bf16)


def _dot(a, b):
    return jnp.dot(a, b, preferred_element_type=_f32)


def _dot_nt(a, b):
    return lax.dot_general(a, b, (((1,), (1,)), ((), ())), preferred_element_type=_f32)


def _dot_tn(a, b):
    return lax.dot_general(a, b, (((0,), (0,)), ((), ())), preferred_element_type=_f32)


def _split_bf16(x, passes):
    pieces = []
    rem = x
    for _ in range(passes):
        piece = _bf(rem)
        pieces.append(piece)
        rem = rem - piece.astype(_f32)
    return pieces


def _dot_split(x, m, passes):
    return sum(_dot(p, m) for p in _split_bf16(x, passes))


def _cumsum_rows(x, tri):
    return sum(_dot(tri, p) for p in _split_bf16(x, 3))


def _rms(x, w):
    return x * lax.rsqrt(jnp.mean(x * x, axis=-1, keepdims=True) + EPS) * w


def _sigmoid(x):
    return 1.0 / (1.0 + jnp.exp(-x))


def _log1p_exp_neg_abs(z):
    return jnp.log(1.0 + jnp.exp(-jnp.abs(z)))


def _softplus(z):
    return jnp.maximum(z, 0.0) + jnp.log(1.0 + jnp.exp2(jnp.abs(z) * (-LOG2E)))


def _div_pow2(x, d):
    shift = d.bit_length() - 1
    assert d == 1 << shift
    return lax.shift_right_logical(x, shift)


def _params(sem, vmem=None):
    return pltpu.CompilerParams(dimension_semantics=sem, vmem_limit_bytes=vmem)


def _resident(shape, index_map):
    return pl.BlockSpec(shape, index_map, pipeline_mode=pl.Buffered(1))


def _cond_kernel(c_ref, w_ref, b_ref, o_ref):
    c = c_ref[...]
    s = c * _sigmoid(c)
    o_ref[...] = _dot(_bf(s), _bf(w_ref[...])) + b_ref[...]


def _cond(c_all, cond_w, cond_b):
    depth = cond_w.shape[0]
    n = c_all.shape[0]
    tn = D_MODEL
    return pl.pallas_call(
        _cond_kernel,
        grid=(depth, N_COND * D_MODEL // tn),
        in_specs=[
            pl.BlockSpec((n, D_MODEL), lambda l, j: (0, 0)),
            pl.BlockSpec((None, D_MODEL, tn), lambda l, j: (l, 0, j)),
            pl.BlockSpec((None, 1, tn), lambda l, j: (l, 0, j)),
        ],
        out_specs=pl.BlockSpec((None, n, tn), lambda l, j: (l, 0, j)),
        out_shape=jax.ShapeDtypeStruct((depth, n, N_COND * D_MODEL), _f32),
        compiler_params=_params(("parallel", "parallel")),
        name="cond",
    )(c_all, cond_w, cond_b.reshape(depth, 1, N_COND * D_MODEL))


class _Cond:
    def __init__(self, m, seq_rows, tm):
        if seq_rows % tm == 0:
            self.arr = m.reshape(m.shape[0], 1, N_COND * D_MODEL)
            per = seq_rows // tm
            self._spec = lambda j: pl.BlockSpec((None, 1, D_MODEL), lambda i: (i // per, 0, j))
        else:
            assert tm % seq_rows == 0
            self.arr = jnp.repeat(m, seq_rows, axis=0)
            self._spec = lambda j: pl.BlockSpec((tm, D_MODEL), lambda i: (i, j))

    def spec(self, sub, kind):
        return self._spec(3 * sub + kind)


def _ffn_kernel(x_ref, sh_ref, sc_ref, gt_ref, npre_ref, npost_ref, win_ref, wout_ref, o_ref,
                a_ref, acc_ref):
    x = x_ref[...]
    a = _rms(x, npre_ref[...]) * (1.0 + sc_ref[...]) + sh_ref[...]
    a_ref[...] = _bf(a)
    for c in range(D_FF // FFN_CHUNK):
        lo = c * FFN_CHUNK
        ab = a_ref[...]
        g = _dot(ab, win_ref[:, lo:lo + FFN_CHUNK])
        u = _dot(ab, win_ref[:, D_FF + lo:D_FF + lo + FFN_CHUNK])
        act = _bf(g * _sigmoid(g) * u)
        y = _dot(act, wout_ref[lo:lo + FFN_CHUNK, :])
        if c == 0:
            acc_ref[...] = y
        else:
            acc_ref[...] += y
    o_ref[...] = x + MACARON_W * gt_ref[...] * _rms(acc_ref[...], npost_ref[...])


def _ffn(x, cond, sub, npre, npost, w_in, w_out, l, idx, tm):
    m = x.shape[0]
    row = lambda i: (i, 0)
    vec = pl.BlockSpec((1, D_MODEL), lambda i: (0, 0))
    return pl.pallas_call(
        _ffn_kernel,
        grid=(m // tm,),
        in_specs=[
            pl.BlockSpec((tm, D_MODEL), row),
            cond.spec(sub, 0), cond.spec(sub, 1), cond.spec(sub, 2),
            vec, vec,
            _resident((None, None, D_MODEL, 2 * D_FF), lambda i: (l, idx, 0, 0)),
            _resident((None, None, D_FF, D_MODEL), lambda i: (l, idx, 0, 0)),
        ],
        out_specs=pl.BlockSpec((tm, D_MODEL), row),
        out_shape=jax.ShapeDtypeStruct((m, D_MODEL), _f32),
        scratch_shapes=[pltpu.VMEM((tm, D_MODEL), _bf16), pltpu.VMEM((tm, D_MODEL), _f32)],
        compiler_params=_params(("parallel",), VMEM_LIMIT),
        name="ffn",
    )(x, cond.arr, cond.arr, cond.arr, npre, npost, w_in, w_out)


def _mixin_kernel(x_ref, sh_ref, sc_ref, npre_ref, w_ref, wt_ref, w2_ref, bg_ref, lnw_ref, lnb_ref,
                  ws_ref, bs_ref,
                  gla_ref, la_ref, ogm_ref, vn_ref, qb_ref, kt_ref, vt_ref, ktb_ref, vtb_ref,
                  *, period):
    tm = x_ref.shape[0]
    tk = ktb_ref.shape[-1]
    x = x_ref[...]
    a = _bf(_rms(x, npre_ref[...]) * (1.0 + sc_ref[...]) + sh_ref[...])

    gla_ref[...] = _dot(a, w_ref[:, COL_GLA:COL_GLA + 4 * GLA_W])
    lr = _dot(a, w_ref[:, COL_LR:COL_LR + LR_PAD])
    zl = _dot(_bf(lr), w2_ref[...]) + bg_ref[...]
    la_ref[...] = (jnp.minimum(zl, 0.0) - _log1p_exp_neg_abs(zl)) * (1.0 / GLA_TAU)

    q = _dot(a, w_ref[:, COL_SBQ:COL_SBQ + SB_W])
    qb_ref[...] = _bf(q * (HEAD_DIM ** -0.5))
    kt = _dot_nt(wt_ref[0:SB_W, :], a)
    vt = _dot_nt(wt_ref[SB_W:2 * SB_W, :], a)
    kt_ref[...] = kt
    vt_ref[...] = vt
    for j in range(tm // tk):
        ktb_ref[j] = _bf(kt[:, j * tk:(j + 1) * tk])
        vtb_ref[j] = _bf(vt[:, j * tk:(j + 1) * tk])

    mu = _dot(a, w_ref[:, COL_GM:COL_GM + GM_W])
    mv = _dot(a, w_ref[:, COL_GM + GM_W:COL_GM + 2 * GM_W])
    mean = jnp.mean(mv, axis=-1, keepdims=True)
    d = mv - mean
    var = jnp.mean(d * d, axis=-1, keepdims=True)
    vn = d * lax.rsqrt(var + EPS) * lnw_ref[...] + lnb_ref[...]
    vn_ref[...] = vn
    vnb = _bf(vn)
    r = lax.broadcasted_iota(jnp.int32, (GM_CHUNK, GM_CHUNK), 0)
    c = lax.broadcasted_iota(jnp.int32, (GM_CHUNK, GM_CHUNK), 1)
    keep = (c <= r) & (_div_pow2(r, period) == _div_pow2(c, period))
    wm = [_bf(jnp.where(keep, ws_ref[h], 0.0)) for h in range(GM_HEADS)]
    col_head = _div_pow2(lax.broadcasted_iota(jnp.int32, (GM_CHUNK, GM_W), 1), HEAD_DIM)
    for ci in range(tm // GM_CHUNK):
        rows = slice(ci * GM_CHUNK, (ci + 1) * GM_CHUNK)
        vc = vnb[rows, :]
        s = bs_ref[...]
        for h in range(GM_HEADS):
            s = s + jnp.where(col_head == h, _dot(wm[h], vc), 0.0)
        ogm_ref[rows, :] = _bf(mu[rows, :] * s)


def _mixin(x, cond, npre, w_nn, w_t, l, w2p, bg, lnw, lnb, ws, bs, tm, period, seq, tk):
    m = x.shape[0]
    n_seq, per = m // seq, seq // tm
    row = lambda i: (i, 0)
    const2 = lambda i: (0, 0)
    rows_out = [(4 * GLA_W, _f32), (GLA_W, _f32), (GM_W, _bf16), (GM_W, _f32), (SB_W, _bf16)]
    t_spec = pl.BlockSpec((None, SB_W, tm), lambda i: (i // per, 0, i % per))
    tb_spec = pl.BlockSpec((None, tm // tk, SB_W, tk), lambda i: (i // per, i % per, 0, 0))
    t_shape = jax.ShapeDtypeStruct((n_seq, SB_W, seq), _f32)
    tb_shape = jax.ShapeDtypeStruct((n_seq, seq // tk, SB_W, tk), _bf16)
    return pl.pallas_call(
        functools.partial(_mixin_kernel, period=period),
        grid=(m // tm,),
        in_specs=[
            pl.BlockSpec((tm, D_MODEL), row),
            cond.spec(1, 0), cond.spec(1, 1),
            pl.BlockSpec((1, D_MODEL), const2),
            _resident((None, D_MODEL, MIX_COLS), lambda i: (l, 0, 0)),
            _resident((None, 2 * SB_W, D_MODEL), lambda i: (l, 0, 0)),
            pl.BlockSpec((LR_PAD, GLA_W), const2),
            pl.BlockSpec((1, GLA_W), const2),
            pl.BlockSpec((1, GM_W), const2),
            pl.BlockSpec((1, GM_W), const2),
            pl.BlockSpec((GM_HEADS, GM_CHUNK, GM_CHUNK), lambda i: (0, 0, 0)),
            pl.BlockSpec((GM_CHUNK, GM_W), const2),
        ],
        out_specs=[pl.BlockSpec((tm, w), row) for w, _ in rows_out] + [t_spec, t_spec, tb_spec, tb_spec],
        out_shape=[jax.ShapeDtypeStruct((m, w), dt) for w, dt in rows_out]
        + [t_shape, t_shape, tb_shape, tb_shape],
        compiler_params=_params(("parallel",), VMEM_LIMIT),
        name="mixin",
    )(x, cond.arr, cond.arr, npre, w_nn, w_t, w2p, bg, lnw, lnb, ws, bs)


def _gla_kernel(gla_ref, la_ref, nw_ref, s0_ref, o_ref, s_ref, st_ref, ob_ref, *, chunk):
    nb, tl = gla_ref.shape[:2]
    j = pl.program_id(1)
    heads = [slice(h * HEAD_DIM, (h + 1) * HEAD_DIM) for h in range(GLA_HEADS)]

    @pl.when(j == 0)
    def _():
        for s in range(nb):
            for h in range(GLA_HEADS):
                st_ref[s, h] = s0_ref[s, h].T

    ri = lax.broadcasted_iota(jnp.int32, (chunk, chunk), 0)
    ci = lax.broadcasted_iota(jnp.int32, (chunk, chunk), 1)
    causal = ci <= ri
    tri = _bf(jnp.where(causal, 1.0, 0.0))
    gi = _div_pow2(lax.broadcasted_iota(jnp.int32, (GLA_W, GLA_W), 0), HEAD_DIM)
    gj = _div_pow2(lax.broadcasted_iota(jnp.int32, (GLA_W, GLA_W), 1), HEAD_DIM)
    group = _bf(jnp.where(gi == gj, 1.0, 0.0))

    n = tl // chunk
    units = [(s, slice(c * chunk, (c + 1) * chunk)) for s in range(nb) for c in range(n)]
    b = [_cumsum_rows(la_ref[s, r, :], tri) for s, r in units]
    q_t, k_t, k_end, decay, v = [], [], [], [], []
    for u, (s, r) in enumerate(units):
        k = gla_ref[s, r, GLA_W:2 * GLA_W]
        b_last = b[u][chunk - 1:chunk, :]
        q_t.append(_bf(gla_ref[s, r, 0:GLA_W] * (HEAD_DIM ** -0.5) * jnp.exp(b[u])))
        k_t.append(_bf(k * jnp.exp(-b[u])))
        k_end.append(_bf(k * jnp.exp(b_last - b[u])))
        decay.append(jnp.exp(b_last))
        v.append(_bf(gla_ref[s, r, 2 * GLA_W:3 * GLA_W]))
    att = [[_bf(jnp.where(causal, _dot_nt(q_t[u][:, hs], k_t[u][:, hs]), 0.0)) for hs in heads]
           for u in range(len(units))]
    kv = [[_dot_tn(v[u][:, hs], k_end[u][:, hs]) for hs in heads] for u in range(len(units))]
    for u, (s, r) in enumerate(units):
        for h, hs in enumerate(heads):
            ob_ref[s, r, hs] = _dot(att[u][h], v[u][:, hs])
    entry = {}
    for s in range(nb):
        for h, hs in enumerate(heads):
            st = st_ref[s, h]
            for c in range(n):
                u = s * n + c
                entry[u, h] = _bf(st)
                st = st * decay[u][:, hs] + kv[u][h]
            st_ref[s, h] = st
    for u, (s, r) in enumerate(units):
        for h, hs in enumerate(heads):
            ob_ref[s, r, hs] += _dot_nt(q_t[u][:, hs], entry[u, h])
    for s, r in units:
        o = ob_ref[s, r, :]
        g = gla_ref[s, r, 3 * GLA_W:4 * GLA_W]
        ms = _dot_split(o * o, group, 2) * (1.0 / HEAD_DIM)
        y = o * lax.rsqrt(ms + EPS) * nw_ref[...]
        o_ref[s, r, :] = _bf(y * (g * _sigmoid(g)))

    @pl.when(j == pl.num_programs(1) - 1)
    def _():
        for s in range(nb):
            for h in range(GLA_HEADS):
                s_ref[s, h] = st_ref[s, h].T


def _gla(gla, la, nw, s0, nb, tl, chunk):
    bsz, length, _ = gla.shape
    state_spec = pl.BlockSpec((nb, GLA_HEADS, HEAD_DIM, HEAD_DIM), lambda b, j: (b, 0, 0, 0))
    return pl.pallas_call(
        functools.partial(_gla_kernel, chunk=chunk),
        grid=(bsz // nb, length // tl),
        in_specs=[
            pl.BlockSpec((nb, tl, 4 * GLA_W), lambda b, j: (b, j, 0)),
            pl.BlockSpec((nb, tl, GLA_W), lambda b, j: (b, j, 0)),
            pl.BlockSpec((1, GLA_W), lambda b, j: (0, 0)),
            state_spec,
        ],
        out_specs=[pl.BlockSpec((nb, tl, GLA_W), lambda b, j: (b, j, 0)), state_spec],
        out_shape=[
            jax.ShapeDtypeStruct((bsz, length, GLA_W), _bf16),
            jax.ShapeDtypeStruct((bsz, GLA_HEADS, HEAD_DIM, HEAD_DIM), _f32),
        ],
        scratch_shapes=[
            pltpu.VMEM((nb, GLA_HEADS, HEAD_DIM, HEAD_DIM), _f32),
            pltpu.VMEM((nb, tl, GLA_W), _f32),
        ],
        compiler_params=_params(("parallel", "arbitrary")),
        name="gla",
    )(gla, la, nw, s0)


def _sb_table(tk, with_total):
    cols = tk + LANES if with_total else tk
    r = lax.broadcasted_iota(jnp.int32, (2 * tk, cols), 0) & (tk - 1)
    c = lax.broadcasted_iota(jnp.int32, (2 * tk, cols), 1)
    return _bf(jnp.where((r > c) | (c >= tk), 1.0, 0.0))


def _sb_weights(z, table, c_run, mask):
    tk = z.shape[1]
    sp = _softplus(z)
    if mask is not None:
        sp = jnp.where(mask, sp, 0.0)
    sums = _dot(jnp.concatenate(_split_bf16(sp, 2), axis=1), table)
    a = jnp.exp((z - sp) - (sums[:, :tk] + c_run))
    if mask is not None:
        a = jnp.where(mask, a, 0.0)
    return _bf(a), sp, sums


def _sbp_kernel(bias_ref, q_ref, k_ref, v_ref, o_ref,
                table_ref, qh_ref, c_ref, acc_ref, zs_ref, sp_ref, a_ref):
    tq = q_ref.shape[0]
    assert k_ref.shape[-1] == tq
    i = pl.program_id(1)
    table_ref[...] = _sb_table(tq, with_total=False)
    heads = [slice(h * HEAD_DIM, (h + 1) * HEAD_DIM) for h in range(SB_HEADS)]
    for h, hs in enumerate(heads):
        qh_ref[h] = q_ref[:, hs]

    def visit(j, mask, first):
        for h, hs in enumerate(heads):
            z = _dot(qh_ref[h], k_ref[j, hs, :]) + bias_ref[h]
            sp = _softplus(z)
            if mask is not None:
                sp = jnp.where(mask, sp, 0.0)
            zs_ref[h] = z - sp
            sp_ref[h] = jnp.concatenate(_split_bf16(sp, 2), axis=1)
        for h in range(SB_HEADS):
            nearer = _dot(sp_ref[h], table_ref[...])
            c_run = jnp.zeros((tq, LANES), _f32) if first else c_ref[h]
            a = jnp.exp(zs_ref[h] - (nearer + jnp.concatenate([c_run] * (tq // LANES), axis=1)))
            if mask is not None:
                a = jnp.where(mask, a, 0.0)
            a_ref[h] = _bf(a)
            first_key = sp_ref[h, :, 0:1].astype(_f32) + sp_ref[h, :, tq:tq + 1].astype(_f32)
            block_sum = jnp.broadcast_to(nearer[:, 0:1] + first_key, (tq, LANES))
            c_ref[h] = block_sum if first else c_run + block_sum
        for h, hs in enumerate(heads):
            av = _dot_nt(a_ref[h], v_ref[j, hs, :])
            acc_ref[h] = av if first else acc_ref[h] + av

    r = lax.broadcasted_iota(jnp.int32, (tq, tq), 0)
    c = lax.broadcasted_iota(jnp.int32, (tq, tq), 1)
    visit(i, c < r, True)

    def body(jj, carry):
        visit(i - 1 - jj, None, False)
        return carry

    lax.fori_loop(0, i, body, 0)
    for h, hs in enumerate(heads):
        o_ref[:, hs] = _bf(acc_ref[h])


def _sb_prompt(qb, ktb, vtb, bias):
    bsz, length, _ = qb.shape
    nblk, _, tq = ktb.shape[1:]
    kv_spec = pl.BlockSpec((None, nblk, SB_W, tq), lambda b, i: (b, 0, 0, 0))
    return pl.pallas_call(
        _sbp_kernel,
        grid=(bsz, nblk),
        in_specs=[
            pl.BlockSpec(memory_space=pltpu.SMEM),
            pl.BlockSpec((None, tq, SB_W), lambda b, i: (b, i, 0)),
            kv_spec, kv_spec,
        ],
        out_specs=pl.BlockSpec((None, tq, SB_W), lambda b, i: (b, i, 0)),
        out_shape=jax.ShapeDtypeStruct((bsz, length, SB_W), _bf16),
        scratch_shapes=[
            pltpu.VMEM((2 * tq, tq), _bf16),
            pltpu.VMEM((SB_HEADS, tq, HEAD_DIM), _bf16),
            pltpu.VMEM((SB_HEADS, tq, LANES), _f32),
            pltpu.VMEM((SB_HEADS, tq, HEAD_DIM), _f32),
            pltpu.VMEM((SB_HEADS, tq, tq), _f32),
            pltpu.VMEM((SB_HEADS, tq, 2 * tq), _bf16),
            pltpu.VMEM((SB_HEADS, tq, tq), _bf16),
        ],
        compiler_params=_params(("parallel", "arbitrary"), VMEM_LIMIT),
        name="sb_prompt",
    )(bias, qb, ktb, vtb)


def _sbs_kernel(pt_ref, bias_ref, qbd_ref, kn_ref, vn_ref, *refs, n_new):
    pp, gs = SAMPLE_PAGES_PER_STEP, SAMPLE_PAGES_PER_GROUP
    k_refs, v_refs = refs[:pp], refs[pp:2 * pp]
    o_ref, c_ref, acc_ref, z_ref, later_ref = refs[2 * pp:]
    g = pl.program_id(1)
    rows = qbd_ref.shape[0]
    n = gs * rows
    table = _sb_table(PAGE_SIZE, with_total=True)
    qbd = qbd_ref[...]
    bias = bias_ref[...]

    @pl.when(g == 0)
    def _():
        ri = lax.broadcasted_iota(jnp.int32, (n, n), 0)
        ci = lax.broadcasted_iota(jnp.int32, (n, n), 1)
        same_row = (ri & (rows - 1)) == (ci & (rows - 1))
        later_ref[...] = _bf(jnp.where(same_row & (_div_pow2(ci, rows) > _div_pow2(ri, rows)), 1.0, 0.0))
        t = _div_pow2(lax.broadcasted_iota(jnp.int32, (rows, PAGE_SIZE), 0), SB_HEADS)
        s = lax.broadcasted_iota(jnp.int32, (rows, PAGE_SIZE), 1)
        a, _, sums = _sb_weights(_dot(qbd, kn_ref[...]) + bias, table,
                                 jnp.zeros((rows, PAGE_SIZE), _f32), s < t)
        acc_ref[...] = _dot_nt(a, vn_ref[...])
        c_ref[...] = sums[:, PAGE_SIZE:]

    groups = [range(k * gs, (k + 1) * gs) for k in reversed(range(pp // gs))]
    for grp in groups:
        for p in grp:
            z_ref[p * rows:(p + 1) * rows, :] = _dot(qbd, _bf(k_refs[p][...])) + bias
    zs, sums = [], []
    for grp in groups:
        z = z_ref[grp[0] * rows:(grp[-1] + 1) * rows, :]
        sp = _softplus(z)
        zs.append(z - sp)
        sums.append(_dot(jnp.concatenate(_split_bf16(sp, 2), axis=1), table))
    c_run = c_ref[...]
    weights = []
    for k in range(len(groups)):
        page_sum = sums[k][:, PAGE_SIZE:]
        nearer = (sum(_dot(later_ref[...], piece) for piece in _split_bf16(page_sum, 2))
                  + jnp.concatenate([c_run] * gs, axis=0))
        weights.append(_bf(jnp.exp(zs[k] - (sums[k][:, :PAGE_SIZE] + nearer))))
        for q in range(gs):
            c_run = c_run + page_sum[q * rows:(q + 1) * rows, :]
    c_ref[...] = c_run
    acc = acc_ref[...]
    for k, grp in enumerate(groups):
        for q, p in enumerate(grp):
            acc = acc + _dot_nt(weights[k][q * rows:(q + 1) * rows, :], _bf(v_refs[p][...]))
    acc_ref[...] = acc

    @pl.when(g == pl.num_programs(1) - 1)
    def _():
        rh = lax.broadcasted_iota(jnp.int32, (rows, SB_W), 0) & (SB_HEADS - 1)
        ch = _div_pow2(lax.broadcasted_iota(jnp.int32, (rows, SB_W), 1), HEAD_DIM)
        own = jnp.where(rh == ch, acc_ref[...], 0.0)
        o_ref[...] = _bf(jnp.sum(own.reshape(n_new, SB_HEADS, SB_W), axis=1))


def _sb_sample(page_table, bias_rows, qbd, knt, vnt, cache_kt, cache_vt, l, n_new):
    db, n_pages = page_table.shape
    pp = SAMPLE_PAGES_PER_STEP
    n_steps = n_pages // pp
    rows = qbd.shape[1]

    def page_spec(i):
        return pl.BlockSpec(
            (None, None, SB_W, PAGE_SIZE),
            lambda b, g, pt: (l, pt[b, (n_steps - 1 - g) * pp + i], 0, 0))

    grid_spec = pltpu.PrefetchScalarGridSpec(
        num_scalar_prefetch=1,
        grid=(db, n_steps),
        in_specs=[
            pl.BlockSpec((rows, LANES), lambda b, g, pt: (0, 0)),
            pl.BlockSpec((None, rows, SB_W), lambda b, g, pt: (b, 0, 0)),
            pl.BlockSpec((None, SB_W, PAGE_SIZE), lambda b, g, pt: (b, 0, 0)),
            pl.BlockSpec((None, SB_W, PAGE_SIZE), lambda b, g, pt: (b, 0, 0)),
        ] + [page_spec(i) for i in range(pp)] * 2,
        out_specs=pl.BlockSpec((None, n_new, SB_W), lambda b, g, pt: (b, 0, 0)),
        scratch_shapes=[
            pltpu.VMEM((rows, LANES), _f32),
            pltpu.VMEM((rows, SB_W), _f32),
            pltpu.VMEM((pp * rows, PAGE_SIZE), _f32),
            pltpu.VMEM((SAMPLE_PAGES_PER_GROUP * rows, SAMPLE_PAGES_PER_GROUP * rows), _bf16),
        ],
    )
    return pl.pallas_call(
        functools.partial(_sbs_kernel, n_new=n_new),
        grid_spec=grid_spec,
        out_shape=jax.ShapeDtypeStruct((db, n_new, SB_W), _bf16),
        compiler_params=_params(("parallel", "arbitrary"), VMEM_LIMIT),
        name="sb_sample",
    )(page_table, bias_rows, qbd, knt, vnt, *([cache_kt] * pp), *([cache_vt] * pp))


def _mixout_kernel(x_ref, gt_ref, npost_ref, og_ref, om_ref, os_ref, w_ref, o_ref):
    y = (_dot(og_ref[...], w_ref[0:GLA_W, :])
         + _dot(om_ref[...], w_ref[GLA_W:GLA_W + GM_W, :])
         + _dot(os_ref[...], w_ref[GLA_W + GM_W:, :]))
    o_ref[...] = x_ref[...] + gt_ref[...] * _rms(y, npost_ref[...])


def _mixout(x, cond, npost, o_gla, o_gm, o_sb, w_out, l, tm):
    m = x.shape[0]
    row = lambda i: (i, 0)
    return pl.pallas_call(
        _mixout_kernel,
        grid=(m // tm,),
        in_specs=[
            pl.BlockSpec((tm, D_MODEL), row),
            cond.spec(1, 2),
            pl.BlockSpec((1, D_MODEL), lambda i: (0, 0)),
            pl.BlockSpec((tm, GLA_W), row),
            pl.BlockSpec((tm, GM_W), row),
            pl.BlockSpec((tm, SB_W), row),
            _resident((None, D_MODEL, D_MODEL), lambda i: (l, 0, 0)),
        ],
        out_specs=pl.BlockSpec((tm, D_MODEL), row),
        out_shape=jax.ShapeDtypeStruct((m, D_MODEL), _f32),
        compiler_params=_params(("parallel",), VMEM_LIMIT),
        name="mixout",
    )(x, cond.arr, npost, o_gla, o_gm, o_sb, w_out)


def kernel(x_prompt, x_sample, c_prompt, c_sample, cache_sb_k, cache_sb_v, state_gla, page_table,
           cond_w, cond_b, norm_pre, norm_post, ffn_w_in, ffn_w_out, mix_w_in, mix_w_out,
           gla_w_gate2, gla_b_gate, gla_norm, gm_ln_w, gm_ln_b, gm_ws, gm_bs, sb_bias):
    depth = cond_w.shape[0]
    bsz, seq, _ = x_prompt.shape
    db, n_new, _ = x_sample.shape
    n_pool = cache_sb_k.shape[1]
    mp, ms = bsz * seq, db * n_new
    assert ms == GM_CHUNK and GM_CHUNK % n_new == 0 and seq % TM_PROMPT == 0

    w_in_b = _bf(ffn_w_in)
    w_out_b = _bf(ffn_w_out)
    mix_out_b = _bf(mix_w_out)
    lr0 = 4 * GLA_W
    gm0 = lr0 + GLA_LOWRANK
    sk0 = gm0 + 2 * GM_W + SB_W
    mix_nn = _bf(jnp.concatenate(
        [mix_w_in[:, :, :lr0], mix_w_in[:, :, gm0:sk0], mix_w_in[:, :, lr0:gm0],
         jnp.zeros((depth, D_MODEL, LR_PAD - GLA_LOWRANK), _f32)], axis=-1))
    mix_t = _bf(jnp.swapaxes(mix_w_in[:, :, sk0:], 1, 2))
    w2_pad = _bf(jnp.pad(gla_w_gate2, ((0, 0), (0, LR_PAD - GLA_LOWRANK), (0, 0))))
    to_t = lambda cch: jnp.transpose(cch, (0, 1, 3, 4, 2)).reshape(depth, n_pool, SB_W, PAGE_SIZE)
    cache_kt, cache_vt = to_t(cache_sb_k), to_t(cache_sb_v)
    head_cols = (jnp.arange(SB_W) // HEAD_DIM)[None, :] == jnp.arange(SB_HEADS)[:, None]

    m_all = _cond(jnp.concatenate([c_prompt, c_sample], axis=0), cond_w, cond_b)

    xp = x_prompt.reshape(mp, D_MODEL)
    xs = x_sample.reshape(ms, D_MODEL)
    s0_prompt = jnp.zeros((bsz, GLA_HEADS, HEAD_DIM, HEAD_DIM), _f32)
    gla_p, gla_s, kp, vp, ksl, vsl, gms = [], [], [], [], [], [], []
    vec = lambda a: a.reshape(1, -1)
    from_t = lambda t: jnp.transpose(t.reshape(t.shape[0], SB_HEADS, HEAD_DIM, -1), (0, 3, 1, 2))

    for l in range(depth):
        npre = [vec(norm_pre[l, i]) for i in range(3)]
        npost = [vec(norm_post[l, i]) for i in range(3)]
        bg, nw = vec(gla_b_gate[l]), vec(gla_norm[l])
        lnw, lnb = vec(gm_ln_w[l]), vec(gm_ln_b[l])

        cond = _Cond(m_all[l, :bsz], seq, TM_PROMPT)
        xp = _ffn(xp, cond, 0, npre[0], npost[0], w_in_b, w_out_b, l, 0, TM_PROMPT)
        bs_rows = jnp.repeat(gm_bs[l].T, HEAD_DIM, axis=1)
        gla, la, o_gm, _, qb, kt, vt, ktb, vtb = _mixin(
            xp, cond, npre[1], mix_nn, mix_t, l, w2_pad[l], bg, lnw, lnb, gm_ws[l], bs_rows,
            TM_PROMPT, GM_CHUNK, seq, SB_TQ)
        o_gla, s_gla = _gla(gla.reshape(bsz, seq, 4 * GLA_W), la.reshape(bsz, seq, GLA_W), nw,
                            s0_prompt, 1, GLA_TL, GLA_CHUNK)
        o_sb = _sb_prompt(qb.reshape(bsz, seq, SB_W), ktb, vtb, sb_bias[l])
        xp = _mixout(xp, cond, npost[1], o_gla.reshape(mp, GLA_W), o_gm, o_sb.reshape(mp, SB_W),
                     mix_out_b, l, TM_PROMPT)
        xp = _ffn(xp, cond, 2, npre[2], npost[2], w_in_b, w_out_b, l, 1, TM_PROMPT)
        gla_p.append(s_gla)
        kp.append(from_t(kt))
        vp.append(from_t(vt))

        cond = _Cond(m_all[l, bsz:], n_new, ms)
        xs = _ffn(xs, cond, 0, npre[0], npost[0], w_in_b, w_out_b, l, 0, ms)
        reps = GM_CHUNK // n_new
        ws_s = jnp.tile(gm_ws[l][:, :n_new, :n_new], (1, reps, reps))
        bs_s = jnp.tile(bs_rows[:n_new], (reps, 1))
        gla, la, o_gm, vn, qb, kt, vt, ktb, vtb = _mixin(
            xs, cond, npre[1], mix_nn, mix_t, l, w2_pad[l], bg, lnw, lnb, ws_s, bs_s,
            ms, n_new, ms, ms)
        pad_rows = lambda a, n: jnp.pad(a.reshape(db, n_new, -1), ((0, 0), (0, n - n_new), (0, 0)))
        o_gla, s_gla = _gla(pad_rows(gla, SUBLANES), pad_rows(la, SUBLANES), nw, state_gla[l],
                            GLA_SAMPLE_SEQS, SUBLANES, SUBLANES)
        qbd = (qb.reshape(db, n_new, 1, SB_W) * head_cols.astype(_bf16)[None, None]
               ).reshape(db, n_new * SB_HEADS, SB_W)
        bias_rows = jnp.broadcast_to(jnp.tile(sb_bias[l], n_new)[:, None], (n_new * SB_HEADS, LANES))
        new_page = lambda t: jnp.pad(jnp.transpose(t.reshape(SB_W, db, n_new), (1, 0, 2)),
                                     ((0, 0), (0, 0), (0, PAGE_SIZE - n_new)))
        o_sb = _sb_sample(page_table, bias_rows, qbd, new_page(ktb), new_page(vtb),
                          cache_kt, cache_vt, l, n_new)
        xs = _mixout(xs, cond, npost[1], o_gla[:, :n_new].reshape(ms, GLA_W), o_gm,
                     o_sb.reshape(ms, SB_W), mix_out_b, l, ms)
        xs = _ffn(xs, cond, 2, npre[2], npost[2], w_in_b, w_out_b, l, 1, ms)
        gla_s.append(s_gla)
        ksl.append(kt.reshape(SB_W, ms).T.reshape(db, n_new, SB_HEADS, HEAD_DIM))
        vsl.append(vt.reshape(SB_W, ms).T.reshape(db, n_new, SB_HEADS, HEAD_DIM))
        gms.append(vn.reshape(db, n_new, GM_W))

    return (xp.reshape(bsz, seq, D_MODEL), xs.reshape(db, n_new, D_MODEL),
            jnp.stack(gla_p), jnp.stack(gla_s), jnp.stack(kp), jnp.stack(vp),
            jnp.stack(ksl), jnp.stack(vsl), jnp.stack(gms))
```

```python
import functools

import jax
import jax.numpy as jnp
from jax import lax
from jax.experimental import pallas as pl
from jax.experimental.pallas import tpu as pltpu

D_MODEL = 1024
HEAD_DIM = 64
GLA_HEADS = 4
GM_HEADS = 4
SB_HEADS = 8
GLA_W = GLA_HEADS * HEAD_DIM
GM_W = GM_HEADS * HEAD_DIM
SB_W = SB_HEADS * HEAD_DIM
GLA_LOWRANK = 16
GLA_TAU = 16.0
GLA_CHUNK = 64
GM_CHUNK = 128
PAGE_SIZE = 128
D_FF = 2816
N_COND = 9
MACARON_W = 0.5
EPS = 1e-6
LOG2E = 1.4426950408889634

LANES = 128
SUBLANES = 8
LR_PAD = LANES
COL_GLA = 0
COL_GM = 4 * GLA_W
COL_SBQ = COL_GM + 2 * GM_W
COL_LR = COL_SBQ + SB_W
MIX_COLS = COL_LR + LR_PAD

FFN_CHUNK = 256
TM_PROMPT = 512
SB_TQ = 256
GLA_TL = 512
GLA_SAMPLE_SEQS = 4
SAMPLE_PAGES_PER_STEP = 16
SAMPLE_PAGES_PER_GROUP = 4
VMEM_LIMIT = 48 * 1024 * 1024

_f32 = jnp.float32
_bf16 = jnp.bfloat16


def _bf(x):
    return x.astype(_bf16)


def _dot(a, b):
    return jnp.dot(a, b, preferred_element_type=_f32)


def _dot_nt(a, b):
    return lax.dot_general(a, b, (((1,), (1,)), ((), ())), preferred_element_type=_f32)


def _dot_tn(a, b):
    return lax.dot_general(a, b, (((0,), (0,)), ((), ())), preferred_element_type=_f32)


def _split_bf16(x, passes):
    pieces = []
    rem = x
    for _ in range(passes):
        piece = _bf(rem)
        pieces.append(piece)
        rem = rem - piece.astype(_f32)
    return pieces


def _dot_split(x, m, passes):
    return sum(_dot(p, m) for p in _split_bf16(x, passes))


def _cumsum_rows(x, tri):
    return sum(_dot(tri, p) for p in _split_bf16(x, 3))


def _rms(x, w):
    return x * lax.rsqrt(jnp.mean(x * x, axis=-1, keepdims=True) + EPS) * w


def _sigmoid(x):
    return 1.0 / (1.0 + jnp.exp(-x))


def _log1p_exp_neg_abs(z):
    return jnp.log(1.0 + jnp.exp(-jnp.abs(z)))


def _softplus(z):
    return jnp.maximum(z, 0.0) + jnp.log(1.0 + jnp.exp2(jnp.abs(z) * (-LOG2E)))


def _div_pow2(x, d):
    shift = d.bit_length() - 1
    assert d == 1 << shift
    return lax.shift_right_logical(x, shift)


def _params(sem, vmem=None):
    return pltpu.CompilerParams(dimension_semantics=sem, vmem_limit_bytes=vmem)


def _resident(shape, index_map):
    return pl.BlockSpec(shape, index_map, pipeline_mode=pl.Buffered(1))


def _cond_kernel(c_ref, w_ref, b_ref, o_ref):
    c = c_ref[...]
    s = c * _sigmoid(c)
    o_ref[...] = _dot(_bf(s), _bf(w_ref[...])) + b_ref[...]


def _cond(c_all, cond_w, cond_b):
    depth = cond_w.shape[0]
    n = c_all.shape[0]
    tn = D_MODEL
    return pl.pallas_call(
        _cond_kernel,
        grid=(depth, N_COND * D_MODEL // tn),
        in_specs=[
            pl.BlockSpec((n, D_MODEL), lambda l, j: (0, 0)),
            pl.BlockSpec((None, D_MODEL, tn), lambda l, j: (l, 0, j)),
            pl.BlockSpec((None, 1, tn), lambda l, j: (l, 0, j)),
        ],
        out_specs=pl.BlockSpec((None, n, tn), lambda l, j: (l, 0, j)),
        out_shape=jax.ShapeDtypeStruct((depth, n, N_COND * D_MODEL), _f32),
        compiler_params=_params(("parallel", "parallel")),
        name="cond",
    )(c_all, cond_w, cond_b.reshape(depth, 1, N_COND * D_MODEL))


class _Cond:
    def __init__(self, m, seq_rows, tm):
        if seq_rows % tm == 0:
            self.arr = m.reshape(m.shape[0], 1, N_COND * D_MODEL)
            per = seq_rows // tm
            self._spec = lambda j: pl.BlockSpec((None, 1, D_MODEL), lambda i: (i // per, 0, j))
        else:
            assert tm % seq_rows == 0
            self.arr = jnp.repeat(m, seq_rows, axis=0)
            self._spec = lambda j: pl.BlockSpec((tm, D_MODEL), lambda i: (i, j))

    def spec(self, sub, kind):
        return self._spec(3 * sub + kind)


def _ffn_kernel(x_ref, sh_ref, sc_ref, gt_ref, npre_ref, npost_ref, win_ref, wout_ref, o_ref,
                a_ref, acc_ref):
    x = x_ref[...]
    a = _rms(x, npre_ref[...]) * (1.0 + sc_ref[...]) + sh_ref[...]
    a_ref[...] = _bf(a)
    for c in range(D_FF // FFN_CHUNK):
        lo = c * FFN_CHUNK
        ab = a_ref[...]
        g = _dot(ab, win_ref[:, lo:lo + FFN_CHUNK])
        u = _dot(ab, win_ref[:, D_FF + lo:D_FF + lo + FFN_CHUNK])
        act = _bf(g * _sigmoid(g) * u)
        y = _dot(act, wout_ref[lo:lo + FFN_CHUNK, :])
        if c == 0:
            acc_ref[...] = y
        else:
            acc_ref[...] += y
    o_ref[...] = x + MACARON_W * gt_ref[...] * _rms(acc_ref[...], npost_ref[...])


def _ffn(x, cond, sub, npre, npost, w_in, w_out, l, idx, tm):
    m = x.shape[0]
    row = lambda i: (i, 0)
    vec = pl.BlockSpec((1, D_MODEL), lambda i: (0, 0))
    return pl.pallas_call(
        _ffn_kernel,
        grid=(m // tm,),
        in_specs=[
            pl.BlockSpec((tm, D_MODEL), row),
            cond.spec(sub, 0), cond.spec(sub, 1), cond.spec(sub, 2),
            vec, vec,
            _resident((None, None, D_MODEL, 2 * D_FF), lambda i: (l, idx, 0, 0)),
            _resident((None, None, D_FF, D_MODEL), lambda i: (l, idx, 0, 0)),
        ],
        out_specs=pl.BlockSpec((tm, D_MODEL), row),
        out_shape=jax.ShapeDtypeStruct((m, D_MODEL), _f32),
        scratch_shapes=[pltpu.VMEM((tm, D_MODEL), _bf16), pltpu.VMEM((tm, D_MODEL), _f32)],
        compiler_params=_params(("parallel",), VMEM_LIMIT),
        name="ffn",
    )(x, cond.arr, cond.arr, cond.arr, npre, npost, w_in, w_out)


def _mixin_kernel(x_ref, sh_ref, sc_ref, npre_ref, w_ref, wt_ref, w2_ref, bg_ref, lnw_ref, lnb_ref,
                  ws_ref, bs_ref, *refs, period):
    earlier = refs[:-9]
    gla_ref, la_ref, ogm_ref, vn_ref, qb_ref, kt_ref, vt_ref, ktb_ref, vtb_ref = refs[-9:]
    tm = x_ref.shape[0]
    tk = ktb_ref.shape[-1]
    x = x_ref[...]
    a = _bf(_rms(x, npre_ref[...]) * (1.0 + sc_ref[...]) + sh_ref[...])

    gla_ref[...] = _dot(a, w_ref[:, COL_GLA:COL_GLA + 4 * GLA_W])
    lr = _dot(a, w_ref[:, COL_LR:COL_LR + LR_PAD])
    zl = _dot(_bf(lr), w2_ref[...]) + bg_ref[...]
    la_ref[...] = (jnp.minimum(zl, 0.0) - _log1p_exp_neg_abs(zl)) * (1.0 / GLA_TAU)

    q = _dot(a, w_ref[:, COL_SBQ:COL_SBQ + SB_W])
    qb_ref[...] = _bf(q * (HEAD_DIM ** -0.5))
    kt = _dot_nt(wt_ref[0:SB_W, :], a)
    vt = _dot_nt(wt_ref[SB_W:2 * SB_W, :], a)
    if kt_ref.shape == kt.shape:
        kt_ref[...] = kt
        vt_ref[...] = vt
    else:
        n_earlier = len(earlier) // 2
        kt_ref[n_earlier] = kt
        vt_ref[n_earlier] = vt
        for d in range(n_earlier):
            kt_ref[d] = earlier[2 * d][...]
            vt_ref[d] = earlier[2 * d + 1][...]
    for j in range(tm // tk):
        ktb_ref[j] = _bf(kt[:, j * tk:(j + 1) * tk])
        vtb_ref[j] = _bf(vt[:, j * tk:(j + 1) * tk])

    mu = _dot(a, w_ref[:, COL_GM:COL_GM + GM_W])
    mv = _dot(a, w_ref[:, COL_GM + GM_W:COL_GM + 2 * GM_W])
    mean = jnp.mean(mv, axis=-1, keepdims=True)
    d = mv - mean
    var = jnp.mean(d * d, axis=-1, keepdims=True)
    vn = d * lax.rsqrt(var + EPS) * lnw_ref[...] + lnb_ref[...]
    vn_ref[...] = vn
    vnb = _bf(vn)
    r = lax.broadcasted_iota(jnp.int32, (GM_CHUNK, GM_CHUNK), 0)
    c = lax.broadcasted_iota(jnp.int32, (GM_CHUNK, GM_CHUNK), 1)
    keep = (c <= r) & (_div_pow2(r, period) == _div_pow2(c, period))
    wm = [_bf(jnp.where(keep, ws_ref[h], 0.0)) for h in range(GM_HEADS)]
    col_head = _div_pow2(lax.broadcasted_iota(jnp.int32, (GM_CHUNK, GM_W), 1), HEAD_DIM)
    for ci in range(tm // GM_CHUNK):
        rows = slice(ci * GM_CHUNK, (ci + 1) * GM_CHUNK)
        vc = vnb[rows, :]
        s = bs_ref[...]
        for h in range(GM_HEADS):
            s = s + jnp.where(col_head == h, _dot(wm[h], vc), 0.0)
        ogm_ref[rows, :] = _bf(mu[rows, :] * s)


def _mixin(x, cond, npre, w_nn, w_t, l, kv_earlier, w2p, bg, lnw, lnb, ws, bs, tm, period, seq, tk):
    m = x.shape[0]
    n_seq, per = m // seq, seq // tm
    extra = [] if kv_earlier is None else list(kv_earlier)
    row = lambda i: (i, 0)
    const2 = lambda i: (0, 0)
    rows_out = [(4 * GLA_W, _f32), (GLA_W, _f32), (GM_W, _bf16), (GM_W, _f32), (SB_W, _bf16)]
    t_in = pl.BlockSpec((None, SB_W, tm), lambda i: (i // per, 0, i % per))
    if kv_earlier is None:
        t_spec, t_shape = t_in, jax.ShapeDtypeStruct((n_seq, SB_W, seq), _f32)
    else:
        depth = len(extra) // 2 + 1
        t_spec = pl.BlockSpec((depth, None, SB_W, tm), lambda i: (0, i // per, 0, i % per))
        t_shape = jax.ShapeDtypeStruct((depth, n_seq, SB_W, seq), _f32)
    tb_spec = pl.BlockSpec((None, tm // tk, SB_W, tk), lambda i: (i // per, i % per, 0, 0))
    tb_shape = jax.ShapeDtypeStruct((n_seq, seq // tk, SB_W, tk), _bf16)
    return pl.pallas_call(
        functools.partial(_mixin_kernel, period=period),
        grid=(m // tm,),
        in_specs=[
            pl.BlockSpec((tm, D_MODEL), row),
            cond.spec(1, 0), cond.spec(1, 1),
            pl.BlockSpec((1, D_MODEL), const2),
            _resident((None, D_MODEL, MIX_COLS), lambda i: (l, 0, 0)),
            _resident((None, 2 * SB_W, D_MODEL), lambda i: (l, 0, 0)),
            pl.BlockSpec((LR_PAD, GLA_W), const2),
            pl.BlockSpec((1, GLA_W), const2),
            pl.BlockSpec((1, GM_W), const2),
            pl.BlockSpec((1, GM_W), const2),
            pl.BlockSpec((GM_HEADS, GM_CHUNK, GM_CHUNK), lambda i: (0, 0, 0)),
            pl.BlockSpec((GM_CHUNK, GM_W), const2),
        ] + [t_in] * len(extra),
        out_specs=[pl.BlockSpec((tm, w), row) for w, _ in rows_out] + [t_spec, t_spec, tb_spec, tb_spec],
        out_shape=[jax.ShapeDtypeStruct((m, w), dt) for w, dt in rows_out]
        + [t_shape, t_shape, tb_shape, tb_shape],
        compiler_params=_params(("parallel",), VMEM_LIMIT),
        name="mixin",
    )(x, cond.arr, cond.arr, npre, w_nn, w_t, w2p, bg, lnw, lnb, ws, bs, *extra)


def _gla_kernel(gla_ref, la_ref, nw_ref, s0_ref, o_ref, s_ref, st_ref, ob_ref, *, chunk):
    nb, tl = gla_ref.shape[:2]
    j = pl.program_id(1)
    heads = [slice(h * HEAD_DIM, (h + 1) * HEAD_DIM) for h in range(GLA_HEADS)]

    @pl.when(j == 0)
    def _():
        for s in range(nb):
            for h in range(GLA_HEADS):
                st_ref[s, h] = s0_ref[s, h].T

    ri = lax.broadcasted_iota(jnp.int32, (chunk, chunk), 0)
    ci = lax.broadcasted_iota(jnp.int32, (chunk, chunk), 1)
    causal = ci <= ri
    tri = _bf(jnp.where(causal, 1.0, 0.0))
    gi = _div_pow2(lax.broadcasted_iota(jnp.int32, (GLA_W, GLA_W), 0), HEAD_DIM)
    gj = _div_pow2(lax.broadcasted_iota(jnp.int32, (GLA_W, GLA_W), 1), HEAD_DIM)
    group = _bf(jnp.where(gi == gj, 1.0, 0.0))

    n = tl // chunk
    units = [(s, slice(c * chunk, (c + 1) * chunk)) for s in range(nb) for c in range(n)]
    b = [_cumsum_rows(la_ref[s, r, :], tri) for s, r in units]
    q_t, k_t, k_end, decay, v = [], [], [], [], []
    for u, (s, r) in enumerate(units):
        k = gla_ref[s, r, GLA_W:2 * GLA_W]
        b_last = b[u][chunk - 1:chunk, :]
        q_t.append(_bf(gla_ref[s, r, 0:GLA_W] * (HEAD_DIM ** -0.5) * jnp.exp(b[u])))
        k_t.append(_bf(k * jnp.exp(-b[u])))
        k_end.append(_bf(k * jnp.exp(b_last - b[u])))
        decay.append(jnp.exp(b_last))
        v.append(_bf(gla_ref[s, r, 2 * GLA_W:3 * GLA_W]))
    att = [[_bf(jnp.where(causal, _dot_nt(q_t[u][:, hs], k_t[u][:, hs]), 0.0)) for hs in heads]
           for u in range(len(units))]
    kv = [[_dot_tn(v[u][:, hs], k_end[u][:, hs]) for hs in heads] for u in range(len(units))]
    for u, (s, r) in enumerate(units):
        for h, hs in enumerate(heads):
            ob_ref[s, r, hs] = _dot(att[u][h], v[u][:, hs])
    entry = {}
    for s in range(nb):
        for h, hs in enumerate(heads):
            st = st_ref[s, h]
            for c in range(n):
                u = s * n + c
                entry[u, h] = _bf(st)
                st = st * decay[u][:, hs] + kv[u][h]
            st_ref[s, h] = st
    for u, (s, r) in enumerate(units):
        for h, hs in enumerate(heads):
            ob_ref[s, r, hs] += _dot_nt(q_t[u][:, hs], entry[u, h])
    for s, r in units:
        o = ob_ref[s, r, :]
        g = gla_ref[s, r, 3 * GLA_W:4 * GLA_W]
        ms = _dot_split(o * o, group, 2) * (1.0 / HEAD_DIM)
        y = o * lax.rsqrt(ms + EPS) * nw_ref[...]
        o_ref[s, r, :] = _bf(y * (g * _sigmoid(g)))

    @pl.when(j == pl.num_programs(1) - 1)
    def _():
        for s in range(nb):
            for h in range(GLA_HEADS):
                s_ref[s, h] = st_ref[s, h].T


def _gla(gla, la, nw, s0, nb, tl, chunk):
    bsz, length, _ = gla.shape
    state_spec = pl.BlockSpec((nb, GLA_HEADS, HEAD_DIM, HEAD_DIM), lambda b, j: (b, 0, 0, 0))
    return pl.pallas_call(
        functools.partial(_gla_kernel, chunk=chunk),
        grid=(bsz // nb, length // tl),
        in_specs=[
            pl.BlockSpec((nb, tl, 4 * GLA_W), lambda b, j: (b, j, 0)),
            pl.BlockSpec((nb, tl, GLA_W), lambda b, j: (b, j, 0)),
            pl.BlockSpec((1, GLA_W), lambda b, j: (0, 0)),
            state_spec,
        ],
        out_specs=[pl.BlockSpec((nb, tl, GLA_W), lambda b, j: (b, j, 0)), state_spec],
        out_shape=[
            jax.ShapeDtypeStruct((bsz, length, GLA_W), _bf16),
            jax.ShapeDtypeStruct((bsz, GLA_HEADS, HEAD_DIM, HEAD_DIM), _f32),
        ],
        scratch_shapes=[
            pltpu.VMEM((nb, GLA_HEADS, HEAD_DIM, HEAD_DIM), _f32),
            pltpu.VMEM((nb, tl, GLA_W), _f32),
        ],
        compiler_params=_params(("parallel", "arbitrary")),
        name="gla",
    )(gla, la, nw, s0)


def _sb_table(tk, with_total):
    cols = tk + LANES if with_total else tk
    r = lax.broadcasted_iota(jnp.int32, (2 * tk, cols), 0) & (tk - 1)
    c = lax.broadcasted_iota(jnp.int32, (2 * tk, cols), 1)
    return _bf(jnp.where((r >= c) | (c >= tk), 1.0, 0.0))


def _sb_weights(z, table, c_run, mask):
    tk = z.shape[1]
    sp = _softplus(z)
    if mask is not None:
        sp = jnp.where(mask, sp, 0.0)
    sums = _dot(jnp.concatenate(_split_bf16(sp, 2), axis=1), table)
    a = jnp.exp(z - (sums[:, :tk] + c_run))
    if mask is not None:
        a = jnp.where(mask, a, 0.0)
    return _bf(a), sums


def _sbp_kernel(bias_ref, q_ref, k_ref, v_ref, o_ref,
                table_ref, qh_ref, c_ref, acc_ref, z_ref, sp_ref, a_ref):
    tq = q_ref.shape[0]
    assert k_ref.shape[-1] == tq
    i = pl.program_id(1)
    table_ref[...] = _sb_table(tq, with_total=False)
    heads = [slice(h * HEAD_DIM, (h + 1) * HEAD_DIM) for h in range(SB_HEADS)]
    for h, hs in enumerate(heads):
        qh_ref[h] = q_ref[:, hs]

    def visit(j, mask, first):
        for h, hs in enumerate(heads):
            z = _dot(qh_ref[h], k_ref[j, hs, :]) + bias_ref[h]
            sp = _softplus(z)
            if mask is not None:
                sp = jnp.where(mask, sp, 0.0)
            z_ref[h] = z
            sp_ref[h] = jnp.concatenate(_split_bf16(sp, 2), axis=1)
        for h in range(SB_HEADS):
            tail = _dot(sp_ref[h], table_ref[...])
            c_run = jnp.zeros((tq, LANES), _f32) if first else c_ref[h]
            a = jnp.exp(z_ref[h] - (tail + jnp.concatenate([c_run] * (tq // LANES), axis=1)))
            if mask is not None:
                a = jnp.where(mask, a, 0.0)
            a_ref[h] = _bf(a)
            block_sum = jnp.broadcast_to(tail[:, 0:1], (tq, LANES))
            c_ref[h] = block_sum if first else c_run + block_sum
        for h, hs in enumerate(heads):
            av = _dot_nt(a_ref[h], v_ref[j, hs, :])
            acc_ref[h] = av if first else acc_ref[h] + av

    r = lax.broadcasted_iota(jnp.int32, (tq, tq), 0)
    c = lax.broadcasted_iota(jnp.int32, (tq, tq), 1)
    visit(i, c < r, True)

    def body(jj, carry):
        visit(i - 1 - jj, None, False)
        return carry

    lax.fori_loop(0, i, body, 0)
    for h, hs in enumerate(heads):
        o_ref[:, hs] = _bf(acc_ref[h])


def _sb_prompt(qb, ktb, vtb, bias):
    bsz, length, _ = qb.shape
    nblk, _, tq = ktb.shape[1:]
    kv_spec = pl.BlockSpec((None, nblk, SB_W, tq), lambda b, i: (b, 0, 0, 0))
    return pl.pallas_call(
        _sbp_kernel,
        grid=(bsz, nblk),
        in_specs=[
            pl.BlockSpec(memory_space=pltpu.SMEM),
            pl.BlockSpec((None, tq, SB_W), lambda b, i: (b, i, 0)),
            kv_spec, kv_spec,
        ],
        out_specs=pl.BlockSpec((None, tq, SB_W), lambda b, i: (b, i, 0)),
        out_shape=jax.ShapeDtypeStruct((bsz, length, SB_W), _bf16),
        scratch_shapes=[
            pltpu.VMEM((2 * tq, tq), _bf16),
            pltpu.VMEM((SB_HEADS, tq, HEAD_DIM), _bf16),
            pltpu.VMEM((SB_HEADS, tq, LANES), _f32),
            pltpu.VMEM((SB_HEADS, tq, HEAD_DIM), _f32),
            pltpu.VMEM((SB_HEADS, tq, tq), _f32),
            pltpu.VMEM((SB_HEADS, tq, 2 * tq), _bf16),
            pltpu.VMEM((SB_HEADS, tq, tq), _bf16),
        ],
        compiler_params=_params(("parallel", "arbitrary"), VMEM_LIMIT),
        name="sb_prompt",
    )(bias, qb, ktb, vtb)


def _sbs_kernel(pt_ref, bias_ref, qbd_ref, kn_ref, vn_ref, *refs, n_new):
    pp, gs = SAMPLE_PAGES_PER_STEP, SAMPLE_PAGES_PER_GROUP
    k_refs, v_refs = refs[:pp], refs[pp:2 * pp]
    o_ref, c_ref, acc_ref, z_ref, later_ref = refs[2 * pp:]
    g = pl.program_id(1)
    rows = qbd_ref.shape[0]
    n = gs * rows
    table = _sb_table(PAGE_SIZE, with_total=True)
    qbd = qbd_ref[...]
    bias = bias_ref[...]

    @pl.when(g == 0)
    def _():
        ri = lax.broadcasted_iota(jnp.int32, (n, n), 0)
        ci = lax.broadcasted_iota(jnp.int32, (n, n), 1)
        same_row = (ri & (rows - 1)) == (ci & (rows - 1))
        later_ref[...] = _bf(jnp.where(same_row & (_div_pow2(ci, rows) > _div_pow2(ri, rows)), 1.0, 0.0))
        t = _div_pow2(lax.broadcasted_iota(jnp.int32, (rows, PAGE_SIZE), 0), SB_HEADS)
        s = lax.broadcasted_iota(jnp.int32, (rows, PAGE_SIZE), 1)
        a, sums = _sb_weights(_dot(qbd, kn_ref[...]) + bias, table,
                                 jnp.zeros((rows, PAGE_SIZE), _f32), s < t)
        acc_ref[...] = _dot_nt(a, vn_ref[...])
        c_ref[...] = sums[:, PAGE_SIZE:]

    groups = [range(k * gs, (k + 1) * gs) for k in reversed(range(pp // gs))]
    for grp in groups:
        for p in grp:
            z_ref[p * rows:(p + 1) * rows, :] = _dot(qbd, _bf(k_refs[p][...])) + bias
    zs, sums = [], []
    for grp in groups:
        zs.append(z_ref[grp[0] * rows:(grp[-1] + 1) * rows, :])
        sums.append(_dot(jnp.concatenate(_split_bf16(_softplus(zs[-1]), 2), axis=1), table))
    c_run = c_ref[...]
    weights = []
    for k in range(len(groups)):
        page_sum = sums[k][:, PAGE_SIZE:]
        nearer = (sum(_dot(later_ref[...], piece) for piece in _split_bf16(page_sum, 2))
                  + jnp.concatenate([c_run] * gs, axis=0))
        weights.append(_bf(jnp.exp(zs[k] - (sums[k][:, :PAGE_SIZE] + nearer))))
        for q in range(gs):
            c_run = c_run + page_sum[q * rows:(q + 1) * rows, :]
    c_ref[...] = c_run
    acc = acc_ref[...]
    for k, grp in enumerate(groups):
        for q, p in enumerate(grp):
            acc = acc + _dot_nt(weights[k][q * rows:(q + 1) * rows, :], _bf(v_refs[p][...]))
    acc_ref[...] = acc

    @pl.when(g == pl.num_programs(1) - 1)
    def _():
        rh = lax.broadcasted_iota(jnp.int32, (rows, SB_W), 0) & (SB_HEADS - 1)
        ch = _div_pow2(lax.broadcasted_iota(jnp.int32, (rows, SB_W), 1), HEAD_DIM)
        own = jnp.where(rh == ch, acc_ref[...], 0.0)
        o_ref[...] = _bf(jnp.sum(own.reshape(n_new, SB_HEADS, SB_W), axis=1))


def _sb_sample(page_table, bias_rows, qbd, knt, vnt, cache_kt, cache_vt, l, n_new):
    db, n_pages = page_table.shape
    pp = SAMPLE_PAGES_PER_STEP
    n_steps = n_pages // pp
    rows = qbd.shape[1]

    def page_spec(i):
        return pl.BlockSpec(
            (None, None, SB_W, PAGE_SIZE),
            lambda b, g, pt: (l, pt[b, (n_steps - 1 - g) * pp + i], 0, 0))

    grid_spec = pltpu.PrefetchScalarGridSpec(
        num_scalar_prefetch=1,
        grid=(db, n_steps),
        in_specs=[
            pl.BlockSpec((rows, LANES), lambda b, g, pt: (0, 0)),
            pl.BlockSpec((None, rows, SB_W), lambda b, g, pt: (b, 0, 0)),
            pl.BlockSpec((None, SB_W, PAGE_SIZE), lambda b, g, pt: (b, 0, 0)),
            pl.BlockSpec((None, SB_W, PAGE_SIZE), lambda b, g, pt: (b, 0, 0)),
        ] + [page_spec(i) for i in range(pp)] * 2,
        out_specs=pl.BlockSpec((None, n_new, SB_W), lambda b, g, pt: (b, 0, 0)),
        scratch_shapes=[
            pltpu.VMEM((rows, LANES), _f32),
            pltpu.VMEM((rows, SB_W), _f32),
            pltpu.VMEM((pp * rows, PAGE_SIZE), _f32),
            pltpu.VMEM((SAMPLE_PAGES_PER_GROUP * rows, SAMPLE_PAGES_PER_GROUP * rows), _bf16),
        ],
    )
    return pl.pallas_call(
        functools.partial(_sbs_kernel, n_new=n_new),
        grid_spec=grid_spec,
        out_shape=jax.ShapeDtypeStruct((db, n_new, SB_W), _bf16),
        compiler_params=_params(("parallel", "arbitrary"), VMEM_LIMIT),
        name="sb_sample",
    )(page_table, bias_rows, qbd, knt, vnt, *([cache_kt] * pp), *([cache_vt] * pp))


def _mixout_kernel(x_ref, gt_ref, npost_ref, og_ref, om_ref, os_ref, w_ref, o_ref):
    y = (_dot(og_ref[...], w_ref[0:GLA_W, :])
         + _dot(om_ref[...], w_ref[GLA_W:GLA_W + GM_W, :])
         + _dot(os_ref[...], w_ref[GLA_W + GM_W:, :]))
    o_ref[...] = x_ref[...] + gt_ref[...] * _rms(y, npost_ref[...])


def _mixout(x, cond, npost, o_gla, o_gm, o_sb, w_out, l, tm):
    m = x.shape[0]
    row = lambda i: (i, 0)
    return pl.pallas_call(
        _mixout_kernel,
        grid=(m // tm,),
        in_specs=[
            pl.BlockSpec((tm, D_MODEL), row),
            cond.spec(1, 2),
            pl.BlockSpec((1, D_MODEL), lambda i: (0, 0)),
            pl.BlockSpec((tm, GLA_W), row),
            pl.BlockSpec((tm, GM_W), row),
            pl.BlockSpec((tm, SB_W), row),
            _resident((None, D_MODEL, D_MODEL), lambda i: (l, 0, 0)),
        ],
        out_specs=pl.BlockSpec((tm, D_MODEL), row),
        out_shape=jax.ShapeDtypeStruct((m, D_MODEL), _f32),
        compiler_params=_params(("parallel",), VMEM_LIMIT),
        name="mixout",
    )(x, cond.arr, npost, o_gla, o_gm, o_sb, w_out)


def kernel(x_prompt, x_sample, c_prompt, c_sample, cache_sb_k, cache_sb_v, state_gla, page_table,
           cond_w, cond_b, norm_pre, norm_post, ffn_w_in, ffn_w_out, mix_w_in, mix_w_out,
           gla_w_gate2, gla_b_gate, gla_norm, gm_ln_w, gm_ln_b, gm_ws, gm_bs, sb_bias):
    depth = cond_w.shape[0]
    bsz, seq, _ = x_prompt.shape
    db, n_new, _ = x_sample.shape
    n_pool = cache_sb_k.shape[1]
    mp, ms = bsz * seq, db * n_new
    assert ms == GM_CHUNK and GM_CHUNK % n_new == 0 and seq % TM_PROMPT == 0

    w_in_b = _bf(ffn_w_in)
    w_out_b = _bf(ffn_w_out)
    mix_out_b = _bf(mix_w_out)
    lr0 = 4 * GLA_W
    gm0 = lr0 + GLA_LOWRANK
    sk0 = gm0 + 2 * GM_W + SB_W
    mix_nn = _bf(jnp.concatenate(
        [mix_w_in[:, :, :lr0], mix_w_in[:, :, gm0:sk0], mix_w_in[:, :, lr0:gm0],
         jnp.zeros((depth, D_MODEL, LR_PAD - GLA_LOWRANK), _f32)], axis=-1))
    mix_t = _bf(jnp.swapaxes(mix_w_in[:, :, sk0:], 1, 2))
    w2_pad = _bf(jnp.pad(gla_w_gate2, ((0, 0), (0, LR_PAD - GLA_LOWRANK), (0, 0))))
    to_t = lambda cch: jnp.transpose(cch, (0, 1, 3, 4, 2)).reshape(depth, n_pool, SB_W, PAGE_SIZE)
    cache_kt, cache_vt = to_t(cache_sb_k), to_t(cache_sb_v)
    head_cols = (jnp.arange(SB_W) // HEAD_DIM)[None, :] == jnp.arange(SB_HEADS)[:, None]

    m_all = _cond(jnp.concatenate([c_prompt, c_sample], axis=0), cond_w, cond_b)

    xp = x_prompt.reshape(mp, D_MODEL)
    xs = x_sample.reshape(ms, D_MODEL)
    s0_prompt = jnp.zeros((bsz, GLA_HEADS, HEAD_DIM, HEAD_DIM), _f32)
    gla_p, gla_s, gms = [], [], []
    kv_p, kv_s = [], []
    vec = lambda a: a.reshape(1, -1)
    from_t = lambda t: jnp.transpose(t.reshape(depth, t.shape[1], SB_HEADS, HEAD_DIM, -1), (0, 1, 4, 2, 3))

    for l in range(depth):
        npre = [vec(norm_pre[l, i]) for i in range(3)]
        npost = [vec(norm_post[l, i]) for i in range(3)]
        bg, nw = vec(gla_b_gate[l]), vec(gla_norm[l])
        lnw, lnb = vec(gm_ln_w[l]), vec(gm_ln_b[l])

        cond = _Cond(m_all[l, :bsz], seq, TM_PROMPT)
        xp = _ffn(xp, cond, 0, npre[0], npost[0], w_in_b, w_out_b, l, 0, TM_PROMPT)
        bs_rows = jnp.repeat(gm_bs[l].T, HEAD_DIM, axis=1)
        last = l == depth - 1
        gla, la, o_gm, _, qb, kt, vt, ktb, vtb = _mixin(
            xp, cond, npre[1], mix_nn, mix_t, l, kv_p if last else None, w2_pad[l], bg, lnw, lnb, gm_ws[l], bs_rows,
            TM_PROMPT, GM_CHUNK, seq, SB_TQ)
        o_gla, s_gla = _gla(gla.reshape(bsz, seq, 4 * GLA_W), la.reshape(bsz, seq, GLA_W), nw,
                            s0_prompt, 1, GLA_TL, GLA_CHUNK)
        o_sb = _sb_prompt(qb.reshape(bsz, seq, SB_W), ktb, vtb, sb_bias[l])
        xp = _mixout(xp, cond, npost[1], o_gla.reshape(mp, GLA_W), o_gm, o_sb.reshape(mp, SB_W),
                     mix_out_b, l, TM_PROMPT)
        xp = _ffn(xp, cond, 2, npre[2], npost[2], w_in_b, w_out_b, l, 1, TM_PROMPT)
        gla_p.append(s_gla)
        kv_p += [kt, vt]

        cond = _Cond(m_all[l, bsz:], n_new, ms)
        xs = _ffn(xs, cond, 0, npre[0], npost[0], w_in_b, w_out_b, l, 0, ms)
        reps = GM_CHUNK // n_new
        ws_s = jnp.tile(gm_ws[l][:, :n_new, :n_new], (1, reps, reps))
        bs_s = jnp.tile(bs_rows[:n_new], (reps, 1))
        gla, la, o_gm, vn, qb, kt, vt, ktb, vtb = _mixin(
            xs, cond, npre[1], mix_nn, mix_t, l, kv_s if last else None, w2_pad[l], bg, lnw, lnb, ws_s, bs_s,
            ms, n_new, ms, ms)
        pad_rows = lambda a, n: jnp.pad(a.reshape(db, n_new, -1), ((0, 0), (0, n - n_new), (0, 0)))
        o_gla, s_gla = _gla(pad_rows(gla, SUBLANES), pad_rows(la, SUBLANES), nw, state_gla[l],
                            GLA_SAMPLE_SEQS, SUBLANES, SUBLANES)
        qbd = (qb.reshape(db, n_new, 1, SB_W) * head_cols.astype(_bf16)[None, None]
               ).reshape(db, n_new * SB_HEADS, SB_W)
        bias_rows = jnp.broadcast_to(jnp.tile(sb_bias[l], n_new)[:, None], (n_new * SB_HEADS, LANES))
        new_page = lambda t: jnp.pad(jnp.transpose(t.reshape(SB_W, db, n_new), (1, 0, 2)),
                                     ((0, 0), (0, 0), (0, PAGE_SIZE - n_new)))
        o_sb = _sb_sample(page_table, bias_rows, qbd, new_page(ktb), new_page(vtb),
                          cache_kt, cache_vt, l, n_new)
        xs = _mixout(xs, cond, npost[1], o_gla[:, :n_new].reshape(ms, GLA_W), o_gm,
                     o_sb.reshape(ms, SB_W), mix_out_b, l, ms)
        xs = _ffn(xs, cond, 2, npre[2], npost[2], w_in_b, w_out_b, l, 1, ms)
        gla_s.append(s_gla)
        kv_s += [kt, vt]
        gms.append(vn.reshape(db, n_new, GM_W))

    kp, vp = (from_t(t) for t in kv_p[-2:])
    ksl, vsl = (from_t(t).reshape(depth, db, n_new, SB_HEADS, HEAD_DIM) for t in kv_s[-2:])
    return (xp.reshape(bsz, seq, D_MODEL), xs.reshape(db, n_new, D_MODEL),
            jnp.stack(gla_p), jnp.stack(gla_s), kp, vp, ksl, vsl, jnp.stack(gms))
```

```python
import functools

import jax
import jax.numpy as jnp
from jax import lax
from jax.experimental import pallas as pl
from jax.experimental.pallas import tpu as pltpu

D_MODEL = 1024
HEAD_DIM = 64
GLA_HEADS = 4
GM_HEADS = 4
SB_HEADS = 8
GLA_W = GLA_HEADS * HEAD_DIM
GM_W = GM_HEADS * HEAD_DIM
SB_W = SB_HEADS * HEAD_DIM
GLA_LOWRANK = 16
GLA_TAU = 16.0
GLA_CHUNK = 64
GM_CHUNK = 128
PAGE_SIZE = 128
D_FF = 2816
N_COND = 9
MACARON_W = 0.5
EPS = 1e-6
LOG2E = 1.4426950408889634

LANES = 128
SUBLANES = 8
LR_PAD = LANES
COL_GLA = 0
COL_GM = 4 * GLA_W
COL_SBQ = COL_GM + 2 * GM_W
COL_LR = COL_SBQ + SB_W
MIX_COLS = COL_LR + LR_PAD

COND_TN = 2304
FFN_CHUNK = 256
TM_PROMPT = 512
SB_TQ = 256
GLA_TL = 1024
GLA_SAMPLE_SEQS = 4
SAMPLE_PAGES_PER_STEP = 32
SAMPLE_PAGES_PER_GROUP = 4
VMEM_LIMIT = 48 * 1024 * 1024

_f32 = jnp.float32
_bf16 = jnp.bfloat16


def _bf(x):
    return x.astype(_bf16)


def _dot(a, b):
    return jnp.dot(a, b, preferred_element_type=_f32)


def _dot_nt(a, b):
    return lax.dot_general(a, b, (((1,), (1,)), ((), ())), preferred_element_type=_f32)


def _dot_tn(a, b):
    return lax.dot_general(a, b, (((0,), (0,)), ((), ())), preferred_element_type=_f32)


def _split_bf16(x, passes):
    pieces = []
    rem = x
    for _ in range(passes):
        piece = _bf(rem)
        pieces.append(piece)
        rem = rem - piece.astype(_f32)
    return pieces


def _dot_split(x, m, passes):
    return sum(_dot(p, m) for p in _split_bf16(x, passes))


def _cumsum_rows(x, tri):
    return sum(_dot(tri, p) for p in _split_bf16(x, 3))


def _rms(x, w):
    return x * lax.rsqrt(jnp.mean(x * x, axis=-1, keepdims=True) + EPS) * w


def _sigmoid(x):
    return 1.0 / (1.0 + jnp.exp(-x))


def _log1p_exp_neg_abs(z):
    return jnp.log(1.0 + jnp.exp(-jnp.abs(z)))


def _softplus(z):
    return jnp.maximum(z, 0.0) + jnp.log(1.0 + jnp.exp2(jnp.abs(z) * (-LOG2E)))


def _div_pow2(x, d):
    shift = d.bit_length() - 1
    assert d == 1 << shift
    return lax.shift_right_logical(x, shift)


def _params(sem, vmem=None):
    return pltpu.CompilerParams(dimension_semantics=sem, vmem_limit_bytes=vmem)


def _resident(shape, index_map):
    return pl.BlockSpec(shape, index_map, pipeline_mode=pl.Buffered(1))


def _cond_kernel(c_ref, w_ref, b_ref, o_ref):
    c = c_ref[...]
    s = c * _sigmoid(c)
    o_ref[...] = _dot(_bf(s), _bf(w_ref[...])) + b_ref[...]


def _cond(c_all, cond_w, cond_b):
    depth = cond_w.shape[0]
    n = c_all.shape[0]
    tn = COND_TN
    return pl.pallas_call(
        _cond_kernel,
        grid=(depth, N_COND * D_MODEL // tn),
        in_specs=[
            pl.BlockSpec((n, D_MODEL), lambda l, j: (0, 0)),
            pl.BlockSpec((None, D_MODEL, tn), lambda l, j: (l, 0, j)),
            pl.BlockSpec((None, 1, tn), lambda l, j: (l, 0, j)),
        ],
        out_specs=pl.BlockSpec((None, n, tn), lambda l, j: (l, 0, j)),
        out_shape=jax.ShapeDtypeStruct((depth, n, N_COND * D_MODEL), _f32),
        compiler_params=_params(("parallel", "parallel"), VMEM_LIMIT),
        name="cond",
    )(c_all, cond_w, cond_b.reshape(depth, 1, N_COND * D_MODEL))


class _Cond:
    def __init__(self, m, seq_rows, tm):
        if seq_rows % tm == 0:
            self.arr = m.reshape(m.shape[0], 1, N_COND * D_MODEL)
            per = seq_rows // tm
            self._spec = lambda j: pl.BlockSpec((None, 1, D_MODEL), lambda i: (i // per, 0, j))
        else:
            assert tm % seq_rows == 0
            self.arr = jnp.repeat(m, seq_rows, axis=0)
            self._spec = lambda j: pl.BlockSpec((tm, D_MODEL), lambda i: (i, j))

    def spec(self, sub, kind):
        return self._spec(3 * sub + kind)


def _ffn_kernel(x_ref, sh_ref, sc_ref, gt_ref, npre_ref, npost_ref, win_ref, wout_ref, o_ref,
                a_ref, acc_ref):
    x = x_ref[...]
    a = _rms(x, npre_ref[...]) * (1.0 + sc_ref[...]) + sh_ref[...]
    a_ref[...] = _bf(a)
    for c in range(D_FF // FFN_CHUNK):
        lo = c * FFN_CHUNK
        ab = a_ref[...]
        g = _dot(ab, win_ref[:, lo:lo + FFN_CHUNK])
        u = _dot(ab, win_ref[:, D_FF + lo:D_FF + lo + FFN_CHUNK])
        act = _bf(g * _sigmoid(g) * u)
        y = _dot(act, wout_ref[lo:lo + FFN_CHUNK, :])
        if c == 0:
            acc_ref[...] = y
        else:
            acc_ref[...] += y
    o_ref[...] = x + MACARON_W * gt_ref[...] * _rms(acc_ref[...], npost_ref[...])


def _ffn(x, cond, sub, npre, npost, w_in, w_out, l, idx, tm):
    m = x.shape[0]
    row = lambda i: (i, 0)
    vec = pl.BlockSpec((1, D_MODEL), lambda i: (0, 0))
    return pl.pallas_call(
        _ffn_kernel,
        grid=(m // tm,),
        in_specs=[
            pl.BlockSpec((tm, D_MODEL), row),
            cond.spec(sub, 0), cond.spec(sub, 1), cond.spec(sub, 2),
            vec, vec,
            _resident((None, None, D_MODEL, 2 * D_FF), lambda i: (l, idx, 0, 0)),
            _resident((None, None, D_FF, D_MODEL), lambda i: (l, idx, 0, 0)),
        ],
        out_specs=pl.BlockSpec((tm, D_MODEL), row),
        out_shape=jax.ShapeDtypeStruct((m, D_MODEL), _f32),
        scratch_shapes=[pltpu.VMEM((tm, D_MODEL), _bf16), pltpu.VMEM((tm, D_MODEL), _f32)],
        compiler_params=_params(("parallel",), VMEM_LIMIT),
        name="ffn",
    )(x, cond.arr, cond.arr, cond.arr, npre, npost, w_in, w_out)


def _mixin_kernel(x_ref, sh_ref, sc_ref, npre_ref, w_ref, wt_ref, w2_ref, bg_ref, lnw_ref, lnb_ref,
                  ws_ref, bs_ref, *refs, period):
    earlier = refs[:-9]
    gla_ref, la_ref, ogm_ref, vn_ref, qb_ref, kt_ref, vt_ref, ktb_ref, vtb_ref = refs[-9:]
    tm = x_ref.shape[0]
    tk = ktb_ref.shape[-1]
    x = x_ref[...]
    a = _bf(_rms(x, npre_ref[...]) * (1.0 + sc_ref[...]) + sh_ref[...])

    gla_ref[...] = _dot(a, w_ref[:, COL_GLA:COL_GLA + 4 * GLA_W])
    lr = _dot(a, w_ref[:, COL_LR:COL_LR + LR_PAD])
    zl = _dot(_bf(lr), w2_ref[...]) + bg_ref[...]
    la_ref[...] = (jnp.minimum(zl, 0.0) - _log1p_exp_neg_abs(zl)) * (1.0 / GLA_TAU)

    q = _dot(a, w_ref[:, COL_SBQ:COL_SBQ + SB_W])
    qb_ref[...] = _bf(q * (HEAD_DIM ** -0.5))
    kt = _dot_nt(wt_ref[0:SB_W, :], a)
    vt = _dot_nt(wt_ref[SB_W:2 * SB_W, :], a)
    if kt_ref.shape == kt.shape:
        kt_ref[...] = kt
        vt_ref[...] = vt
    else:
        n_earlier = len(earlier) // 2
        kt_ref[n_earlier] = kt
        vt_ref[n_earlier] = vt
        for d in range(n_earlier):
            kt_ref[d] = earlier[2 * d][...]
            vt_ref[d] = earlier[2 * d + 1][...]
    for j in range(tm // tk):
        ktb_ref[j] = _bf(kt[:, j * tk:(j + 1) * tk])
        vtb_ref[j] = _bf(vt[:, j * tk:(j + 1) * tk])

    mu = _dot(a, w_ref[:, COL_GM:COL_GM + GM_W])
    mv = _dot(a, w_ref[:, COL_GM + GM_W:COL_GM + 2 * GM_W])
    mean = jnp.mean(mv, axis=-1, keepdims=True)
    d = mv - mean
    var = jnp.mean(d * d, axis=-1, keepdims=True)
    vn = d * lax.rsqrt(var + EPS) * lnw_ref[...] + lnb_ref[...]
    vn_ref[...] = vn
    vnb = _bf(vn)
    r = lax.broadcasted_iota(jnp.int32, (GM_CHUNK, GM_CHUNK), 0)
    c = lax.broadcasted_iota(jnp.int32, (GM_CHUNK, GM_CHUNK), 1)
    keep = (c <= r) & (_div_pow2(r, period) == _div_pow2(c, period))
    wm = [_bf(jnp.where(keep, ws_ref[h], 0.0)) for h in range(GM_HEADS)]
    col_head = _div_pow2(lax.broadcasted_iota(jnp.int32, (GM_CHUNK, GM_W), 1), HEAD_DIM)
    for ci in range(tm // GM_CHUNK):
        rows = slice(ci * GM_CHUNK, (ci + 1) * GM_CHUNK)
        vc = vnb[rows, :]
        s = bs_ref[...]
        for h in range(GM_HEADS):
            s = s + jnp.where(col_head == h, _dot(wm[h], vc), 0.0)
        ogm_ref[rows, :] = _bf(mu[rows, :] * s)


def _mixin(x, cond, npre, w_nn, w_t, l, kv_earlier, w2p, bg, lnw, lnb, ws, bs, tm, period, seq, tk):
    m = x.shape[0]
    n_seq, per = m // seq, seq // tm
    extra = [] if kv_earlier is None else list(kv_earlier)
    row = lambda i: (i, 0)
    const2 = lambda i: (0, 0)
    rows_out = [(4 * GLA_W, _f32), (GLA_W, _f32), (GM_W, _bf16), (GM_W, _f32), (SB_W, _bf16)]
    t_in = pl.BlockSpec((None, SB_W, tm), lambda i: (i // per, 0, i % per))
    if kv_earlier is None:
        t_spec, t_shape = t_in, jax.ShapeDtypeStruct((n_seq, SB_W, seq), _f32)
    else:
        depth = len(extra) // 2 + 1
        t_spec = pl.BlockSpec((depth, None, SB_W, tm), lambda i: (0, i // per, 0, i % per))
        t_shape = jax.ShapeDtypeStruct((depth, n_seq, SB_W, seq), _f32)
    tb_spec = pl.BlockSpec((None, tm // tk, SB_W, tk), lambda i: (i // per, i % per, 0, 0))
    tb_shape = jax.ShapeDtypeStruct((n_seq, seq // tk, SB_W, tk), _bf16)
    return pl.pallas_call(
        functools.partial(_mixin_kernel, period=period),
        grid=(m // tm,),
        in_specs=[
            pl.BlockSpec((tm, D_MODEL), row),
            cond.spec(1, 0), cond.spec(1, 1),
            pl.BlockSpec((1, D_MODEL), const2),
            _resident((None, D_MODEL, MIX_COLS), lambda i: (l, 0, 0)),
            _resident((None, 2 * SB_W, D_MODEL), lambda i: (l, 0, 0)),
            pl.BlockSpec((LR_PAD, GLA_W), const2),
            pl.BlockSpec((1, GLA_W), const2),
            pl.BlockSpec((1, GM_W), const2),
            pl.BlockSpec((1, GM_W), const2),
            pl.BlockSpec((GM_HEADS, GM_CHUNK, GM_CHUNK), lambda i: (0, 0, 0)),
            pl.BlockSpec((GM_CHUNK, GM_W), const2),
        ] + [t_in] * len(extra),
        out_specs=[pl.BlockSpec((tm, w), row) for w, _ in rows_out] + [t_spec, t_spec, tb_spec, tb_spec],
        out_shape=[jax.ShapeDtypeStruct((m, w), dt) for w, dt in rows_out]
        + [t_shape, t_shape, tb_shape, tb_shape],
        compiler_params=_params(("parallel",), VMEM_LIMIT),
        name="mixin",
    )(x, cond.arr, cond.arr, npre, w_nn, w_t, w2p, bg, lnw, lnb, ws, bs, *extra)


def _gla_kernel(gla_ref, la_ref, nw_ref, s0_ref, o_ref, s_ref, st_ref, ob_ref, *, chunk):
    nb, tl = gla_ref.shape[:2]
    j = pl.program_id(1)
    heads = [slice(h * HEAD_DIM, (h + 1) * HEAD_DIM) for h in range(GLA_HEADS)]

    @pl.when(j == 0)
    def _():
        for s in range(nb):
            for h in range(GLA_HEADS):
                st_ref[s, h] = s0_ref[s, h].T

    ri = lax.broadcasted_iota(jnp.int32, (chunk, chunk), 0)
    ci = lax.broadcasted_iota(jnp.int32, (chunk, chunk), 1)
    causal = ci <= ri
    tri = _bf(jnp.where(causal, 1.0, 0.0))
    gi = _div_pow2(lax.broadcasted_iota(jnp.int32, (GLA_W, GLA_W), 0), HEAD_DIM)
    gj = _div_pow2(lax.broadcasted_iota(jnp.int32, (GLA_W, GLA_W), 1), HEAD_DIM)
    group = _bf(jnp.where(gi == gj, 1.0, 0.0))

    n = tl // chunk
    units = [(s, slice(c * chunk, (c + 1) * chunk)) for s in range(nb) for c in range(n)]
    b = [_cumsum_rows(la_ref[s, r, :], tri) for s, r in units]
    q_t, k_t, k_end, decay, v = [], [], [], [], []
    for u, (s, r) in enumerate(units):
        k = gla_ref[s, r, GLA_W:2 * GLA_W]
        b_last = b[u][chunk - 1:chunk, :]
        q_t.append(_bf(gla_ref[s, r, 0:GLA_W] * (HEAD_DIM ** -0.5) * jnp.exp(b[u])))
        k_t.append(_bf(k * jnp.exp(-b[u])))
        k_end.append(_bf(k * jnp.exp(b_last - b[u])))
        decay.append(jnp.exp(b_last))
        v.append(_bf(gla_ref[s, r, 2 * GLA_W:3 * GLA_W]))
    att = [[_bf(jnp.where(causal, _dot_nt(q_t[u][:, hs], k_t[u][:, hs]), 0.0)) for hs in heads]
           for u in range(len(units))]
    kv = [[_dot_tn(v[u][:, hs], k_end[u][:, hs]) for hs in heads] for u in range(len(units))]
    for u, (s, r) in enumerate(units):
        for h, hs in enumerate(heads):
            ob_ref[s, r, hs] = _dot(att[u][h], v[u][:, hs])
    entry = {}
    for s in range(nb):
        for h, hs in enumerate(heads):
            st = st_ref[s, h]
            for c in range(n):
                u = s * n + c
                entry[u, h] = _bf(st)
                st = st * decay[u][:, hs] + kv[u][h]
            st_ref[s, h] = st
    for u, (s, r) in enumerate(units):
        for h, hs in enumerate(heads):
            ob_ref[s, r, hs] += _dot_nt(q_t[u][:, hs], entry[u, h])
    for s, r in units:
        o = ob_ref[s, r, :]
        g = gla_ref[s, r, 3 * GLA_W:4 * GLA_W]
        ms = _dot_split(o * o, group, 2) * (1.0 / HEAD_DIM)
        y = o * lax.rsqrt(ms + EPS) * nw_ref[...]
        o_ref[s, r, :] = _bf(y * (g * _sigmoid(g)))

    @pl.when(j == pl.num_programs(1) - 1)
    def _():
        for s in range(nb):
            for h in range(GLA_HEADS):
                s_ref[s, h] = st_ref[s, h].T


def _gla(gla, la, nw, s0, nb, tl, chunk):
    bsz, length, _ = gla.shape
    state_spec = pl.BlockSpec((nb, GLA_HEADS, HEAD_DIM, HEAD_DIM), lambda b, j: (b, 0, 0, 0))
    return pl.pallas_call(
        functools.partial(_gla_kernel, chunk=chunk),
        grid=(bsz // nb, length // tl),
        in_specs=[
            pl.BlockSpec((nb, tl, 4 * GLA_W), lambda b, j: (b, j, 0)),
            pl.BlockSpec((nb, tl, GLA_W), lambda b, j: (b, j, 0)),
            pl.BlockSpec((1, GLA_W), lambda b, j: (0, 0)),
            state_spec,
        ],
        out_specs=[pl.BlockSpec((nb, tl, GLA_W), lambda b, j: (b, j, 0)), state_spec],
        out_shape=[
            jax.ShapeDtypeStruct((bsz, length, GLA_W), _bf16),
            jax.ShapeDtypeStruct((bsz, GLA_HEADS, HEAD_DIM, HEAD_DIM), _f32),
        ],
        scratch_shapes=[
            pltpu.VMEM((nb, GLA_HEADS, HEAD_DIM, HEAD_DIM), _f32),
            pltpu.VMEM((nb, tl, GLA_W), _f32),
        ],
        compiler_params=_params(("parallel", "arbitrary")),
        name="gla",
    )(gla, la, nw, s0)


def _sb_table(tk, with_total):
    cols = tk + LANES if with_total else tk
    r = lax.broadcasted_iota(jnp.int32, (2 * tk, cols), 0) & (tk - 1)
    c = lax.broadcasted_iota(jnp.int32, (2 * tk, cols), 1)
    return _bf(jnp.where((r >= c) | (c >= tk), 1.0, 0.0))


def _sb_weights(z, table, c_run, mask):
    tk = z.shape[1]
    sp = _softplus(z)
    if mask is not None:
        sp = jnp.where(mask, sp, 0.0)
    sums = _dot(jnp.concatenate(_split_bf16(sp, 2), axis=1), table)
    a = jnp.exp(z - (sums[:, :tk] + c_run))
    if mask is not None:
        a = jnp.where(mask, a, 0.0)
    return _bf(a), sums


def _sbp_kernel(bias_ref, q_ref, k_ref, v_ref, o_ref,
                table_ref, qh_ref, c_ref, acc_ref, z_ref, sp_ref, a_ref):
    tq = q_ref.shape[0]
    assert k_ref.shape[-1] == tq
    i = pl.program_id(1)
    table_ref[...] = _sb_table(tq, with_total=False)
    heads = [slice(h * HEAD_DIM, (h + 1) * HEAD_DIM) for h in range(SB_HEADS)]
    for h, hs in enumerate(heads):
        qh_ref[h] = q_ref[:, hs]

    def visit(j, mask, first):
        for h, hs in enumerate(heads):
            z = _dot(qh_ref[h], k_ref[j, hs, :]) + bias_ref[h]
            sp = _softplus(z)
            if mask is not None:
                sp = jnp.where(mask, sp, 0.0)
            z_ref[h] = z
            sp_ref[h] = jnp.concatenate(_split_bf16(sp, 2), axis=1)
        for h in range(SB_HEADS):
            tail = _dot(sp_ref[h], table_ref[...])
            c_run = jnp.zeros((tq, LANES), _f32) if first else c_ref[h]
            a = jnp.exp(z_ref[h] - (tail + jnp.concatenate([c_run] * (tq // LANES), axis=1)))
            if mask is not None:
                a = jnp.where(mask, a, 0.0)
            a_ref[h] = _bf(a)
            block_sum = jnp.broadcast_to(tail[:, 0:1], (tq, LANES))
            c_ref[h] = block_sum if first else c_run + block_sum
        for h, hs in enumerate(heads):
            av = _dot_nt(a_ref[h], v_ref[j, hs, :])
            acc_ref[h] = av if first else acc_ref[h] + av

    r = lax.broadcasted_iota(jnp.int32, (tq, tq), 0)
    c = lax.broadcasted_iota(jnp.int32, (tq, tq), 1)
    visit(i, c < r, True)

    def body(jj, carry):
        visit(i - 1 - jj, None, False)
        return carry

    lax.fori_loop(0, i, body, 0)
    for h, hs in enumerate(heads):
        o_ref[:, hs] = _bf(acc_ref[h])


def _sb_prompt(qb, ktb, vtb, bias):
    bsz, length, _ = qb.shape
    nblk, _, tq = ktb.shape[1:]
    kv_spec = pl.BlockSpec((None, nblk, SB_W, tq), lambda b, i: (b, 0, 0, 0))
    return pl.pallas_call(
        _sbp_kernel,
        grid=(bsz, nblk),
        in_specs=[
            pl.BlockSpec(memory_space=pltpu.SMEM),
            pl.BlockSpec((None, tq, SB_W), lambda b, i: (b, i, 0)),
            kv_spec, kv_spec,
        ],
        out_specs=pl.BlockSpec((None, tq, SB_W), lambda b, i: (b, i, 0)),
        out_shape=jax.ShapeDtypeStruct((bsz, length, SB_W), _bf16),
        scratch_shapes=[
            pltpu.VMEM((2 * tq, tq), _bf16),
            pltpu.VMEM((SB_HEADS, tq, HEAD_DIM), _bf16),
            pltpu.VMEM((SB_HEADS, tq, LANES), _f32),
            pltpu.VMEM((SB_HEADS, tq, HEAD_DIM), _f32),
            pltpu.VMEM((SB_HEADS, tq, tq), _f32),
            pltpu.VMEM((SB_HEADS, tq, 2 * tq), _bf16),
            pltpu.VMEM((SB_HEADS, tq, tq), _bf16),
        ],
        compiler_params=_params(("parallel", "arbitrary"), VMEM_LIMIT),
        name="sb_prompt",
    )(bias, qb, ktb, vtb)


def _sbs_kernel(pt_ref, bias_ref, qbd_ref, kn_ref, vn_ref, *refs, n_new):
    pp, gs = SAMPLE_PAGES_PER_STEP, SAMPLE_PAGES_PER_GROUP
    k_refs, v_refs = refs[:pp], refs[pp:2 * pp]
    o_ref, c_ref, acc_ref, z_ref, later_ref = refs[2 * pp:]
    g = pl.program_id(1)
    rows = qbd_ref.shape[0]
    n = gs * rows
    table = _sb_table(PAGE_SIZE, with_total=True)
    qbd = qbd_ref[...]
    bias = bias_ref[...]

    @pl.when(g == 0)
    def _():
        ri = lax.broadcasted_iota(jnp.int32, (n, n), 0)
        ci = lax.broadcasted_iota(jnp.int32, (n, n), 1)
        same_row = (ri & (rows - 1)) == (ci & (rows - 1))
        later_ref[...] = _bf(jnp.where(same_row & (_div_pow2(ci, rows) > _div_pow2(ri, rows)), 1.0, 0.0))
        t = _div_pow2(lax.broadcasted_iota(jnp.int32, (rows, PAGE_SIZE), 0), SB_HEADS)
        s = lax.broadcasted_iota(jnp.int32, (rows, PAGE_SIZE), 1)
        a, sums = _sb_weights(_dot(qbd, kn_ref[...]) + bias, table,
                                 jnp.zeros((rows, PAGE_SIZE), _f32), s < t)
        acc_ref[...] = _dot_nt(a, vn_ref[...])
        c_ref[...] = sums[:, PAGE_SIZE:]

    groups = [range(k * gs, (k + 1) * gs) for k in reversed(range(pp // gs))]
    for grp in groups:
        for p in grp:
            z_ref[p * rows:(p + 1) * rows, :] = _dot(qbd, _bf(k_refs[p][...])) + bias
    zs, sums = [], []
    for grp in groups:
        zs.append(z_ref[grp[0] * rows:(grp[-1] + 1) * rows, :])
        sums.append(_dot(jnp.concatenate(_split_bf16(_softplus(zs[-1]), 2), axis=1), table))
    c_run = c_ref[...]
    weights = []
    for k in range(len(groups)):
        page_sum = sums[k][:, PAGE_SIZE:]
        nearer = (sum(_dot(later_ref[...], piece) for piece in _split_bf16(page_sum, 2))
                  + jnp.concatenate([c_run] * gs, axis=0))
        weights.append(_bf(jnp.exp(zs[k] - (sums[k][:, :PAGE_SIZE] + nearer))))
        for q in range(gs):
            c_run = c_run + page_sum[q * rows:(q + 1) * rows, :]
    c_ref[...] = c_run
    acc = acc_ref[...]
    for k, grp in enumerate(groups):
        for q, p in enumerate(grp):
            acc = acc + _dot_nt(weights[k][q * rows:(q + 1) * rows, :], _bf(v_refs[p][...]))
    acc_ref[...] = acc

    @pl.when(g == pl.num_programs(1) - 1)
    def _():
        rh = lax.broadcasted_iota(jnp.int32, (rows, SB_W), 0) & (SB_HEADS - 1)
        ch = _div_pow2(lax.broadcasted_iota(jnp.int32, (rows, SB_W), 1), HEAD_DIM)
        own = jnp.where(rh == ch, acc_ref[...], 0.0)
        o_ref[...] = _bf(jnp.sum(own.reshape(n_new, SB_HEADS, SB_W), axis=1))


def _sb_sample(page_table, bias_rows, qbd, knt, vnt, cache_kt, cache_vt, l, n_new):
    db, n_pages = page_table.shape
    pp = SAMPLE_PAGES_PER_STEP
    n_steps = n_pages // pp
    rows = qbd.shape[1]

    def page_spec(i):
        return pl.BlockSpec(
            (None, None, SB_W, PAGE_SIZE),
            lambda b, g, pt: (l, pt[b, (n_steps - 1 - g) * pp + i], 0, 0))

    grid_spec = pltpu.PrefetchScalarGridSpec(
        num_scalar_prefetch=1,
        grid=(db, n_steps),
        in_specs=[
            pl.BlockSpec((rows, LANES), lambda b, g, pt: (0, 0)),
            pl.BlockSpec((None, rows, SB_W), lambda b, g, pt: (b, 0, 0)),
            pl.BlockSpec((None, SB_W, PAGE_SIZE), lambda b, g, pt: (b, 0, 0)),
            pl.BlockSpec((None, SB_W, PAGE_SIZE), lambda b, g, pt: (b, 0, 0)),
        ] + [page_spec(i) for i in range(pp)] * 2,
        out_specs=pl.BlockSpec((None, n_new, SB_W), lambda b, g, pt: (b, 0, 0)),
        scratch_shapes=[
            pltpu.VMEM((rows, LANES), _f32),
            pltpu.VMEM((rows, SB_W), _f32),
            pltpu.VMEM((pp * rows, PAGE_SIZE), _f32),
            pltpu.VMEM((SAMPLE_PAGES_PER_GROUP * rows, SAMPLE_PAGES_PER_GROUP * rows), _bf16),
        ],
    )
    return pl.pallas_call(
        functools.partial(_sbs_kernel, n_new=n_new),
        grid_spec=grid_spec,
        out_shape=jax.ShapeDtypeStruct((db, n_new, SB_W), _bf16),
        compiler_params=_params(("parallel", "arbitrary"), VMEM_LIMIT),
        name="sb_sample",
    )(page_table, bias_rows, qbd, knt, vnt, *([cache_kt] * pp), *([cache_vt] * pp))


def _mixout_kernel(x_ref, gt_ref, npost_ref, og_ref, om_ref, os_ref, w_ref, o_ref):
    y = (_dot(og_ref[...], w_ref[0:GLA_W, :])
         + _dot(om_ref[...], w_ref[GLA_W:GLA_W + GM_W, :])
         + _dot(os_ref[...], w_ref[GLA_W + GM_W:, :]))
    o_ref[...] = x_ref[...] + gt_ref[...] * _rms(y, npost_ref[...])


def _mixout(x, cond, npost, o_gla, o_gm, o_sb, w_out, l, tm):
    m = x.shape[0]
    row = lambda i: (i, 0)
    return pl.pallas_call(
        _mixout_kernel,
        grid=(m // tm,),
        in_specs=[
            pl.BlockSpec((tm, D_MODEL), row),
            cond.spec(1, 2),
            pl.BlockSpec((1, D_MODEL), lambda i: (0, 0)),
            pl.BlockSpec((tm, GLA_W), row),
            pl.BlockSpec((tm, GM_W), row),
            pl.BlockSpec((tm, SB_W), row),
            _resident((None, D_MODEL, D_MODEL), lambda i: (l, 0, 0)),
        ],
        out_specs=pl.BlockSpec((tm, D_MODEL), row),
        out_shape=jax.ShapeDtypeStruct((m, D_MODEL), _f32),
        compiler_params=_params(("parallel",), VMEM_LIMIT),
        name="mixout",
    )(x, cond.arr, npost, o_gla, o_gm, o_sb, w_out)


def kernel(x_prompt, x_sample, c_prompt, c_sample, cache_sb_k, cache_sb_v, state_gla, page_table,
           cond_w, cond_b, norm_pre, norm_post, ffn_w_in, ffn_w_out, mix_w_in, mix_w_out,
           gla_w_gate2, gla_b_gate, gla_norm, gm_ln_w, gm_ln_b, gm_ws, gm_bs, sb_bias):
    depth = cond_w.shape[0]
    bsz, seq, _ = x_prompt.shape
    db, n_new, _ = x_sample.shape
    n_pool = cache_sb_k.shape[1]
    mp, ms = bsz * seq, db * n_new
    assert ms == GM_CHUNK and GM_CHUNK % n_new == 0 and seq % TM_PROMPT == 0

    w_in_b = _bf(ffn_w_in)
    w_out_b = _bf(ffn_w_out)
    mix_out_b = _bf(mix_w_out)
    lr0 = 4 * GLA_W
    gm0 = lr0 + GLA_LOWRANK
    sk0 = gm0 + 2 * GM_W + SB_W
    mix_nn = _bf(jnp.concatenate(
        [mix_w_in[:, :, :lr0], mix_w_in[:, :, gm0:sk0], mix_w_in[:, :, lr0:gm0],
         jnp.zeros((depth, D_MODEL, LR_PAD - GLA_LOWRANK), _f32)], axis=-1))
    mix_t = _bf(jnp.swapaxes(mix_w_in[:, :, sk0:], 1, 2))
    w2_pad = _bf(jnp.pad(gla_w_gate2, ((0, 0), (0, LR_PAD - GLA_LOWRANK), (0, 0))))
    to_t = lambda cch: jnp.transpose(cch, (0, 1, 3, 4, 2)).reshape(depth, n_pool, SB_W, PAGE_SIZE)
    cache_kt, cache_vt = to_t(cache_sb_k), to_t(cache_sb_v)
    head_cols = (jnp.arange(SB_W) // HEAD_DIM)[None, :] == jnp.arange(SB_HEADS)[:, None]

    m_all = _cond(jnp.concatenate([c_prompt, c_sample], axis=0), cond_w, cond_b)

    xp = x_prompt.reshape(mp, D_MODEL)
    xs = x_sample.reshape(ms, D_MODEL)
    s0_prompt = jnp.zeros((bsz, GLA_HEADS, HEAD_DIM, HEAD_DIM), _f32)
    gla_p, gla_s, gms = [], [], []
    kv_p, kv_s = [], []
    vec = lambda a: a.reshape(1, -1)
    from_t = lambda t: jnp.transpose(t.reshape(depth, t.shape[1], SB_HEADS, HEAD_DIM, -1), (0, 1, 4, 2, 3))

    for l in range(depth):
        npre = [vec(norm_pre[l, i]) for i in range(3)]
        npost = [vec(norm_post[l, i]) for i in range(3)]
        bg, nw = vec(gla_b_gate[l]), vec(gla_norm[l])
        lnw, lnb = vec(gm_ln_w[l]), vec(gm_ln_b[l])

        cond = _Cond(m_all[l, :bsz], seq, TM_PROMPT)
        xp = _ffn(xp, cond, 0, npre[0], npost[0], w_in_b, w_out_b, l, 0, TM_PROMPT)
        bs_rows = jnp.repeat(gm_bs[l].T, HEAD_DIM, axis=1)
        last = l == depth - 1
        gla, la, o_gm, _, qb, kt, vt, ktb, vtb = _mixin(
            xp, cond, npre[1], mix_nn, mix_t, l, kv_p if last else None, w2_pad[l], bg, lnw, lnb, gm_ws[l], bs_rows,
            TM_PROMPT, GM_CHUNK, seq, SB_TQ)
        o_gla, s_gla = _gla(gla.reshape(bsz, seq, 4 * GLA_W), la.reshape(bsz, seq, GLA_W), nw,
                            s0_prompt, 1, GLA_TL, GLA_CHUNK)
        o_sb = _sb_prompt(qb.reshape(bsz, seq, SB_W), ktb, vtb, sb_bias[l])
        xp = _mixout(xp, cond, npost[1], o_gla.reshape(mp, GLA_W), o_gm, o_sb.reshape(mp, SB_W),
                     mix_out_b, l, TM_PROMPT)
        xp = _ffn(xp, cond, 2, npre[2], npost[2], w_in_b, w_out_b, l, 1, TM_PROMPT)
        gla_p.append(s_gla)
        kv_p += [kt, vt]

        cond = _Cond(m_all[l, bsz:], n_new, ms)
        xs = _ffn(xs, cond, 0, npre[0], npost[0], w_in_b, w_out_b, l, 0, ms)
        reps = GM_CHUNK // n_new
        ws_s = jnp.tile(gm_ws[l][:, :n_new, :n_new], (1, reps, reps))
        bs_s = jnp.tile(bs_rows[:n_new], (reps, 1))
        gla, la, o_gm, vn, qb, kt, vt, ktb, vtb = _mixin(
            xs, cond, npre[1], mix_nn, mix_t, l, kv_s if last else None, w2_pad[l], bg, lnw, lnb, ws_s, bs_s,
            ms, n_new, ms, ms)
        pad_rows = lambda a, n: jnp.pad(a.reshape(db, n_new, -1), ((0, 0), (0, n - n_new), (0, 0)))
        o_gla, s_gla = _gla(pad_rows(gla, SUBLANES), pad_rows(la, SUBLANES), nw, state_gla[l],
                            GLA_SAMPLE_SEQS, SUBLANES, SUBLANES)
        qbd = (qb.reshape(db, n_new, 1, SB_W) * head_cols.astype(_bf16)[None, None]
               ).reshape(db, n_new * SB_HEADS, SB_W)
        bias_rows = jnp.broadcast_to(jnp.tile(sb_bias[l], n_new)[:, None], (n_new * SB_HEADS, LANES))
        new_page = lambda t: jnp.pad(jnp.transpose(t.reshape(SB_W, db, n_new), (1, 0, 2)),
                                     ((0, 0), (0, 0), (0, PAGE_SIZE - n_new)))
        o_sb = _sb_sample(page_table, bias_rows, qbd, new_page(ktb), new_page(vtb),
                          cache_kt, cache_vt, l, n_new)
        xs = _mixout(xs, cond, npost[1], o_gla[:, :n_new].reshape(ms, GLA_W), o_gm,
                     o_sb.reshape(ms, SB_W), mix_out_b, l, ms)
        xs = _ffn(xs, cond, 2, npre[2], npost[2], w_in_b, w_out_b, l, 1, ms)
        gla_s.append(s_gla)
        kv_s += [kt, vt]
        gms.append(vn.reshape(db, n_new, GM_W))

    kp, vp = (from_t(t) for t in kv_p[-2:])
    ksl, vsl = (from_t(t).reshape(depth, db, n_new, SB_HEADS, HEAD_DIM) for t in kv_s[-2:])
    return (xp.reshape(bsz, seq, D_MODEL), xs.reshape(db, n_new, D_MODEL),
            jnp.stack(gla_p), jnp.stack(gla_s), kp, vp, ksl, vsl, jnp.stack(gms))
```

```python
import functools

import jax
import jax.numpy as jnp
from jax import lax
from jax.experimental import pallas as pl
from jax.experimental.pallas import tpu as pltpu

D_MODEL = 1024
HEAD_DIM = 64
GLA_HEADS = 4
GM_HEADS = 4
SB_HEADS = 8
GLA_W = GLA_HEADS * HEAD_DIM
GM_W = GM_HEADS * HEAD_DIM
SB_W = SB_HEADS * HEAD_DIM
GLA_LOWRANK = 16
GLA_TAU = 16.0
GLA_CHUNK = 64
GM_CHUNK = 128
PAGE_SIZE = 128
D_FF = 2816
N_COND = 9
MACARON_W = 0.5
EPS = 1e-6
LOG2E = 1.4426950408889634

LANES = 128
SUBLANES = 8
LR_PAD = LANES
ROW_GLA = 0
ROW_LR = 4 * GLA_W
ROW_GM = ROW_LR + GLA_LOWRANK
ROW_SB = ROW_GM + 2 * GM_W
MIX_ROWS = ROW_SB + 3 * SB_W

COND_TN = 2304
FFN_CHUNK = 256
TM_PROMPT = 512
SB_TQ = 256
GLA_TL = 1024
GLA_SAMPLE_SEQS = 4
SAMPLE_PAGES_PER_STEP = 32
SAMPLE_PAGES_PER_GROUP = 4
VMEM_LIMIT = 48 * 1024 * 1024

_f32 = jnp.float32
_bf16 = jnp.bfloat16


def _bf(x):
    return x.astype(_bf16)


def _dot(a, b):
    return jnp.dot(a, b, preferred_element_type=_f32)


def _dot_nt(a, b):
    return lax.dot_general(a, b, (((1,), (1,)), ((), ())), preferred_element_type=_f32)


def _dot_tn(a, b):
    return lax.dot_general(a, b, (((0,), (0,)), ((), ())), preferred_element_type=_f32)


def _split_bf16(x, passes):
    pieces = []
    rem = x
    for _ in range(passes):
        piece = _bf(rem)
        pieces.append(piece)
        rem = rem - piece.astype(_f32)
    return pieces


def _dot_split(x, m, passes):
    return sum(_dot(p, m) for p in _split_bf16(x, passes))


def _cumsum_rows(x, tri):
    return sum(_dot(tri, p) for p in _split_bf16(x, 3))


def _rms(x, w):
    return x * lax.rsqrt(jnp.mean(x * x, axis=-1, keepdims=True) + EPS) * w


def _sigmoid(x):
    return 1.0 / (1.0 + jnp.exp(-x))


def _log1p_exp_neg_abs(z):
    return jnp.log(1.0 + jnp.exp(-jnp.abs(z)))


def _softplus(z):
    return jnp.maximum(z, 0.0) + jnp.log(1.0 + jnp.exp2(jnp.abs(z) * (-LOG2E)))


def _div_pow2(x, d):
    shift = d.bit_length() - 1
    assert d == 1 << shift
    return lax.shift_right_logical(x, shift)


def _params(sem, vmem=None):
    return pltpu.CompilerParams(dimension_semantics=sem, vmem_limit_bytes=vmem)


def _resident(shape, index_map):
    return pl.BlockSpec(shape, index_map, pipeline_mode=pl.Buffered(1))


def _cond_kernel(c_ref, w_ref, b_ref, o_ref):
    c = c_ref[...]
    s = c * _sigmoid(c)
    o_ref[...] = _dot(_bf(s), _bf(w_ref[...])) + b_ref[...]


def _cond(c_all, cond_w, cond_b):
    depth = cond_w.shape[0]
    n = c_all.shape[0]
    tn = COND_TN
    return pl.pallas_call(
        _cond_kernel,
        grid=(depth, N_COND * D_MODEL // tn),
        in_specs=[
            pl.BlockSpec((n, D_MODEL), lambda l, j: (0, 0)),
            pl.BlockSpec((None, D_MODEL, tn), lambda l, j: (l, 0, j)),
            pl.BlockSpec((None, 1, tn), lambda l, j: (l, 0, j)),
        ],
        out_specs=pl.BlockSpec((None, n, tn), lambda l, j: (l, 0, j)),
        out_shape=jax.ShapeDtypeStruct((depth, n, N_COND * D_MODEL), _f32),
        compiler_params=_params(("parallel", "parallel"), VMEM_LIMIT),
        name="cond",
    )(c_all, cond_w, cond_b.reshape(depth, 1, N_COND * D_MODEL))


class _Cond:
    def __init__(self, m, seq_rows, tm):
        if seq_rows % tm == 0:
            self.arr = m.reshape(m.shape[0], 1, N_COND * D_MODEL)
            per = seq_rows // tm
            self._spec = lambda j: pl.BlockSpec((None, 1, D_MODEL), lambda i: (i // per, 0, j))
        else:
            assert tm % seq_rows == 0
            self.arr = jnp.repeat(m, seq_rows, axis=0)
            self._spec = lambda j: pl.BlockSpec((tm, D_MODEL), lambda i: (i, j))

    def spec(self, sub, kind):
        return self._spec(3 * sub + kind)


def _ffn_kernel(x_ref, sh_ref, sc_ref, gt_ref, npre_ref, npost_ref, win_ref, wout_ref, o_ref,
                a_ref, acc_ref):
    x = x_ref[...]
    a = _rms(x, npre_ref[...]) * (1.0 + sc_ref[...]) + sh_ref[...]
    a_ref[...] = _bf(a)
    for c in range(D_FF // FFN_CHUNK):
        lo = c * FFN_CHUNK
        ab = a_ref[...]
        g = _dot(ab, win_ref[:, lo:lo + FFN_CHUNK])
        u = _dot(ab, win_ref[:, D_FF + lo:D_FF + lo + FFN_CHUNK])
        act = _bf(g * _sigmoid(g) * u)
        y = _dot(act, wout_ref[lo:lo + FFN_CHUNK, :])
        if c == 0:
            acc_ref[...] = y
        else:
            acc_ref[...] += y
    o_ref[...] = x + MACARON_W * gt_ref[...] * _rms(acc_ref[...], npost_ref[...])


def _ffn(x, cond, sub, npre, npost, w_in, w_out, l, idx, tm):
    m = x.shape[0]
    row = lambda i: (i, 0)
    vec = pl.BlockSpec((1, D_MODEL), lambda i: (0, 0))
    return pl.pallas_call(
        _ffn_kernel,
        grid=(m // tm,),
        in_specs=[
            pl.BlockSpec((tm, D_MODEL), row),
            cond.spec(sub, 0), cond.spec(sub, 1), cond.spec(sub, 2),
            vec, vec,
            _resident((None, None, D_MODEL, 2 * D_FF), lambda i: (l, idx, 0, 0)),
            _resident((None, None, D_FF, D_MODEL), lambda i: (l, idx, 0, 0)),
        ],
        out_specs=pl.BlockSpec((tm, D_MODEL), row),
        out_shape=jax.ShapeDtypeStruct((m, D_MODEL), _f32),
        scratch_shapes=[pltpu.VMEM((tm, D_MODEL), _bf16), pltpu.VMEM((tm, D_MODEL), _f32)],
        compiler_params=_params(("parallel",), VMEM_LIMIT),
        name="ffn",
    )(x, cond.arr, cond.arr, cond.arr, npre, npost, w_in, w_out)


def _mixin_kernel(x_ref, sh_ref, sc_ref, npre_ref, wt_ref, w2_ref, bg_ref, lnw_ref, lnb_ref,
                  ws_ref, bs_ref, *refs, period):
    earlier = refs[:-9]
    gla_ref, la_ref, ogm_ref, vn_ref, qb_ref, kt_ref, vt_ref, ktb_ref, vtb_ref = refs[-9:]
    tm = x_ref.shape[0]
    tk = ktb_ref.shape[-1]
    x = x_ref[...]
    a = _bf(_rms(x, npre_ref[...]) * (1.0 + sc_ref[...]) + sh_ref[...])

    gla_ref[...] = _dot_nt(a, wt_ref[ROW_GLA:ROW_GLA + 4 * GLA_W, :])
    lr = _dot_nt(a, wt_ref[ROW_LR:ROW_LR + LR_PAD, :])
    zl = _dot(_bf(lr), w2_ref[...]) + bg_ref[...]
    la_ref[...] = (jnp.minimum(zl, 0.0) - _log1p_exp_neg_abs(zl)) * (1.0 / GLA_TAU)

    q = _dot_nt(a, wt_ref[ROW_SB:ROW_SB + SB_W, :])
    qb_ref[...] = _bf(q * (HEAD_DIM ** -0.5))
    kt = _dot_nt(wt_ref[ROW_SB + SB_W:ROW_SB + 2 * SB_W, :], a)
    vt = _dot_nt(wt_ref[ROW_SB + 2 * SB_W:ROW_SB + 3 * SB_W, :], a)
    if kt_ref.shape == kt.shape:
        kt_ref[...] = kt
        vt_ref[...] = vt
    else:
        n_earlier = len(earlier) // 2
        kt_ref[n_earlier] = kt
        vt_ref[n_earlier] = vt
        for d in range(n_earlier):
            kt_ref[d] = earlier[2 * d][...]
            vt_ref[d] = earlier[2 * d + 1][...]
    for j in range(tm // tk):
        ktb_ref[j] = _bf(kt[:, j * tk:(j + 1) * tk])
        vtb_ref[j] = _bf(vt[:, j * tk:(j + 1) * tk])

    mu = _dot_nt(a, wt_ref[ROW_GM:ROW_GM + GM_W, :])
    mv = _dot_nt(a, wt_ref[ROW_GM + GM_W:ROW_GM + 2 * GM_W, :])
    mean = jnp.mean(mv, axis=-1, keepdims=True)
    d = mv - mean
    var = jnp.mean(d * d, axis=-1, keepdims=True)
    vn = d * lax.rsqrt(var + EPS) * lnw_ref[...] + lnb_ref[...]
    vn_ref[...] = vn
    vnb = _bf(vn)
    r = lax.broadcasted_iota(jnp.int32, (GM_CHUNK, GM_CHUNK), 0)
    c = lax.broadcasted_iota(jnp.int32, (GM_CHUNK, GM_CHUNK), 1)
    keep = (c <= r) & (_div_pow2(r, period) == _div_pow2(c, period))
    wm = [_bf(jnp.where(keep, ws_ref[h], 0.0)) for h in range(GM_HEADS)]
    col_head = _div_pow2(lax.broadcasted_iota(jnp.int32, (GM_CHUNK, GM_W), 1), HEAD_DIM)
    for ci in range(tm // GM_CHUNK):
        rows = slice(ci * GM_CHUNK, (ci + 1) * GM_CHUNK)
        vc = vnb[rows, :]
        s = bs_ref[...]
        for h in range(GM_HEADS):
            s = s + jnp.where(col_head == h, _dot(wm[h], vc), 0.0)
        ogm_ref[rows, :] = _bf(mu[rows, :] * s)


def _mixin(x, cond, npre, w_t, l, kv_earlier, w2p, bg, lnw, lnb, ws, bs, tm, period, seq, tk):
    m = x.shape[0]
    n_seq, per = m // seq, seq // tm
    extra = [] if kv_earlier is None else list(kv_earlier)
    row = lambda i: (i, 0)
    const2 = lambda i: (0, 0)
    rows_out = [(4 * GLA_W, _f32), (GLA_W, _f32), (GM_W, _bf16), (GM_W, _f32), (SB_W, _bf16)]
    t_in = pl.BlockSpec((None, SB_W, tm), lambda i: (i // per, 0, i % per))
    if kv_earlier is None:
        t_spec, t_shape = t_in, jax.ShapeDtypeStruct((n_seq, SB_W, seq), _f32)
    else:
        depth = len(extra) // 2 + 1
        t_spec = pl.BlockSpec((depth, None, SB_W, tm), lambda i: (0, i // per, 0, i % per))
        t_shape = jax.ShapeDtypeStruct((depth, n_seq, SB_W, seq), _f32)
    tb_spec = pl.BlockSpec((None, tm // tk, SB_W, tk), lambda i: (i // per, i % per, 0, 0))
    tb_shape = jax.ShapeDtypeStruct((n_seq, seq // tk, SB_W, tk), _bf16)
    return pl.pallas_call(
        functools.partial(_mixin_kernel, period=period),
        grid=(m // tm,),
        in_specs=[
            pl.BlockSpec((tm, D_MODEL), row),
            cond.spec(1, 0), cond.spec(1, 1),
            pl.BlockSpec((1, D_MODEL), const2),
            _resident((None, MIX_ROWS, D_MODEL), lambda i: (l, 0, 0)),
            pl.BlockSpec((LR_PAD, GLA_W), const2),
            pl.BlockSpec((1, GLA_W), const2),
            pl.BlockSpec((1, GM_W), const2),
            pl.BlockSpec((1, GM_W), const2),
            pl.BlockSpec((GM_HEADS, GM_CHUNK, GM_CHUNK), lambda i: (0, 0, 0)),
            pl.BlockSpec((GM_CHUNK, GM_W), const2),
        ] + [t_in] * len(extra),
        out_specs=[pl.BlockSpec((tm, w), row) for w, _ in rows_out] + [t_spec, t_spec, tb_spec, tb_spec],
        out_shape=[jax.ShapeDtypeStruct((m, w), dt) for w, dt in rows_out]
        + [t_shape, t_shape, tb_shape, tb_shape],
        compiler_params=_params(("parallel",), VMEM_LIMIT),
        name="mixin",
    )(x, cond.arr, cond.arr, npre, w_t, w2p, bg, lnw, lnb, ws, bs, *extra)


def _gla_kernel(gla_ref, la_ref, nw_ref, s0_ref, o_ref, s_ref, st_ref, ob_ref, *, chunk):
    nb, tl = gla_ref.shape[:2]
    j = pl.program_id(1)
    heads = [slice(h * HEAD_DIM, (h + 1) * HEAD_DIM) for h in range(GLA_HEADS)]

    @pl.when(j == 0)
    def _():
        for s in range(nb):
            for h in range(GLA_HEADS):
                st_ref[s, h] = s0_ref[s, h].T

    ri = lax.broadcasted_iota(jnp.int32, (chunk, chunk), 0)
    ci = lax.broadcasted_iota(jnp.int32, (chunk, chunk), 1)
    causal = ci <= ri
    tri = _bf(jnp.where(causal, 1.0, 0.0))
    gi = _div_pow2(lax.broadcasted_iota(jnp.int32, (GLA_W, GLA_W), 0), HEAD_DIM)
    gj = _div_pow2(lax.broadcasted_iota(jnp.int32, (GLA_W, GLA_W), 1), HEAD_DIM)
    group = _bf(jnp.where(gi == gj, 1.0, 0.0))

    n = tl // chunk
    units = [(s, slice(c * chunk, (c + 1) * chunk)) for s in range(nb) for c in range(n)]
    b = [_cumsum_rows(la_ref[s, r, :], tri) for s, r in units]
    q_t, k_t, k_end, decay, v = [], [], [], [], []
    for u, (s, r) in enumerate(units):
        k = gla_ref[s, r, GLA_W:2 * GLA_W]
        b_last = b[u][chunk - 1:chunk, :]
        q_t.append(_bf(gla_ref[s, r, 0:GLA_W] * (HEAD_DIM ** -0.5) * jnp.exp(b[u])))
        k_t.append(_bf(k * jnp.exp(-b[u])))
        k_end.append(_bf(k * jnp.exp(b_last - b[u])))
        decay.append(jnp.exp(b_last))
        v.append(_bf(gla_ref[s, r, 2 * GLA_W:3 * GLA_W]))
    att = [[_bf(jnp.where(causal, _dot_nt(q_t[u][:, hs], k_t[u][:, hs]), 0.0)) for hs in heads]
           for u in range(len(units))]
    kv = [[_dot_tn(v[u][:, hs], k_end[u][:, hs]) for hs in heads] for u in range(len(units))]
    for u, (s, r) in enumerate(units):
        for h, hs in enumerate(heads):
            ob_ref[s, r, hs] = _dot(att[u][h], v[u][:, hs])
    entry = {}
    for s in range(nb):
        for h, hs in enumerate(heads):
            st = st_ref[s, h]
            for c in range(n):
                u = s * n + c
                entry[u, h] = _bf(st)
                st = st * decay[u][:, hs] + kv[u][h]
            st_ref[s, h] = st
    for u, (s, r) in enumerate(units):
        for h, hs in enumerate(heads):
            ob_ref[s, r, hs] += _dot_nt(q_t[u][:, hs], entry[u, h])
    for s, r in units:
        o = ob_ref[s, r, :]
        g = gla_ref[s, r, 3 * GLA_W:4 * GLA_W]
        ms = _dot_split(o * o, group, 2) * (1.0 / HEAD_DIM)
        y = o * lax.rsqrt(ms + EPS) * nw_ref[...]
        o_ref[s, r, :] = _bf(y * (g * _sigmoid(g)))

    @pl.when(j == pl.num_programs(1) - 1)
    def _():
        for s in range(nb):
            for h in range(GLA_HEADS):
                s_ref[s, h] = st_ref[s, h].T


def _gla(gla, la, nw, s0, nb, tl, chunk):
    bsz, length, _ = gla.shape
    state_spec = pl.BlockSpec((nb, GLA_HEADS, HEAD_DIM, HEAD_DIM), lambda b, j: (b, 0, 0, 0))
    return pl.pallas_call(
        functools.partial(_gla_kernel, chunk=chunk),
        grid=(bsz // nb, length // tl),
        in_specs=[
            pl.BlockSpec((nb, tl, 4 * GLA_W), lambda b, j: (b, j, 0)),
            pl.BlockSpec((nb, tl, GLA_W), lambda b, j: (b, j, 0)),
            pl.BlockSpec((1, GLA_W), lambda b, j: (0, 0)),
            state_spec,
        ],
        out_specs=[pl.BlockSpec((nb, tl, GLA_W), lambda b, j: (b, j, 0)), state_spec],
        out_shape=[
            jax.ShapeDtypeStruct((bsz, length, GLA_W), _bf16),
            jax.ShapeDtypeStruct((bsz, GLA_HEADS, HEAD_DIM, HEAD_DIM), _f32),
        ],
        scratch_shapes=[
            pltpu.VMEM((nb, GLA_HEADS, HEAD_DIM, HEAD_DIM), _f32),
            pltpu.VMEM((nb, tl, GLA_W), _f32),
        ],
        compiler_params=_params(("parallel", "arbitrary")),
        name="gla",
    )(gla, la, nw, s0)


def _sb_table(tk, with_total):
    cols = tk + LANES if with_total else tk
    r = lax.broadcasted_iota(jnp.int32, (2 * tk, cols), 0) & (tk - 1)
    c = lax.broadcasted_iota(jnp.int32, (2 * tk, cols), 1)
    return _bf(jnp.where((r >= c) | (c >= tk), 1.0, 0.0))


def _sb_weights(z, table, c_run, mask):
    tk = z.shape[1]
    sp = _softplus(z)
    if mask is not None:
        sp = jnp.where(mask, sp, 0.0)
    sums = _dot(jnp.concatenate(_split_bf16(sp, 2), axis=1), table)
    a = jnp.exp(z - (sums[:, :tk] + c_run))
    if mask is not None:
        a = jnp.where(mask, a, 0.0)
    return _bf(a), sums


def _sbp_kernel(bias_ref, q_ref, k_ref, v_ref, o_ref,
                table_ref, qh_ref, c_ref, acc_ref, z_ref, sp_ref, a_ref):
    tq = q_ref.shape[0]
    assert k_ref.shape[-1] == tq
    i = pl.program_id(1)
    table_ref[...] = _sb_table(tq, with_total=False)
    heads = [slice(h * HEAD_DIM, (h + 1) * HEAD_DIM) for h in range(SB_HEADS)]
    for h, hs in enumerate(heads):
        qh_ref[h] = q_ref[:, hs]

    def visit(j, mask, first):
        for h, hs in enumerate(heads):
            z = _dot(qh_ref[h], k_ref[j, hs, :]) + bias_ref[h]
            sp = _softplus(z)
            if mask is not None:
                sp = jnp.where(mask, sp, 0.0)
            z_ref[h] = z
            sp_ref[h] = jnp.concatenate(_split_bf16(sp, 2), axis=1)
        for h in range(SB_HEADS):
            tail = _dot(sp_ref[h], table_ref[...])
            c_run = jnp.zeros((tq, LANES), _f32) if first else c_ref[h]
            a = jnp.exp(z_ref[h] - (tail + jnp.concatenate([c_run] * (tq // LANES), axis=1)))
            if mask is not None:
                a = jnp.where(mask, a, 0.0)
            a_ref[h] = _bf(a)
            block_sum = jnp.broadcast_to(tail[:, 0:1], (tq, LANES))
            c_ref[h] = block_sum if first else c_run + block_sum
        for h, hs in enumerate(heads):
            av = _dot_nt(a_ref[h], v_ref[j, hs, :])
            acc_ref[h] = av if first else acc_ref[h] + av

    r = lax.broadcasted_iota(jnp.int32, (tq, tq), 0)
    c = lax.broadcasted_iota(jnp.int32, (tq, tq), 1)
    visit(i, c < r, True)

    def body(jj, carry):
        visit(i - 1 - jj, None, False)
        return carry

    lax.fori_loop(0, i, body, 0)
    for h, hs in enumerate(heads):
        o_ref[:, hs] = _bf(acc_ref[h])


def _sb_prompt(qb, ktb, vtb, bias):
    bsz, length, _ = qb.shape
    nblk, _, tq = ktb.shape[1:]
    kv_spec = pl.BlockSpec((None, nblk, SB_W, tq), lambda b, i: (b, 0, 0, 0))
    return pl.pallas_call(
        _sbp_kernel,
        grid=(bsz, nblk),
        in_specs=[
            pl.BlockSpec(memory_space=pltpu.SMEM),
            pl.BlockSpec((None, tq, SB_W), lambda b, i: (b, i, 0)),
            kv_spec, kv_spec,
        ],
        out_specs=pl.BlockSpec((None, tq, SB_W), lambda b, i: (b, i, 0)),
        out_shape=jax.ShapeDtypeStruct((bsz, length, SB_W), _bf16),
        scratch_shapes=[
            pltpu.VMEM((2 * tq, tq), _bf16),
            pltpu.VMEM((SB_HEADS, tq, HEAD_DIM), _bf16),
            pltpu.VMEM((SB_HEADS, tq, LANES), _f32),
            pltpu.VMEM((SB_HEADS, tq, HEAD_DIM), _f32),
            pltpu.VMEM((SB_HEADS, tq, tq), _f32),
            pltpu.VMEM((SB_HEADS, tq, 2 * tq), _bf16),
            pltpu.VMEM((SB_HEADS, tq, tq), _bf16),
        ],
        compiler_params=_params(("parallel", "arbitrary"), VMEM_LIMIT),
        name="sb_prompt",
    )(bias, qb, ktb, vtb)


def _sbs_kernel(pt_ref, bias_ref, qbd_ref, kn_ref, vn_ref, *refs, n_new):
    pp, gs = SAMPLE_PAGES_PER_STEP, SAMPLE_PAGES_PER_GROUP
    k_refs, v_refs = refs[:pp], refs[pp:2 * pp]
    o_ref, c_ref, acc_ref, z_ref, later_ref = refs[2 * pp:]
    g = pl.program_id(1)
    rows = qbd_ref.shape[0]
    n = gs * rows
    table = _sb_table(PAGE_SIZE, with_total=True)
    qbd = qbd_ref[...]
    bias = bias_ref[...]

    @pl.when(g == 0)
    def _():
        ri = lax.broadcasted_iota(jnp.int32, (n, n), 0)
        ci = lax.broadcasted_iota(jnp.int32, (n, n), 1)
        same_row = (ri & (rows - 1)) == (ci & (rows - 1))
        later_ref[...] = _bf(jnp.where(same_row & (_div_pow2(ci, rows) > _div_pow2(ri, rows)), 1.0, 0.0))
        t = _div_pow2(lax.broadcasted_iota(jnp.int32, (rows, PAGE_SIZE), 0), SB_HEADS)
        s = lax.broadcasted_iota(jnp.int32, (rows, PAGE_SIZE), 1)
        a, sums = _sb_weights(_dot(qbd, kn_ref[...]) + bias, table,
                                 jnp.zeros((rows, PAGE_SIZE), _f32), s < t)
        acc_ref[...] = _dot_nt(a, vn_ref[...])
        c_ref[...] = sums[:, PAGE_SIZE:]

    groups = [range(k * gs, (k + 1) * gs) for k in reversed(range(pp // gs))]
    for grp in groups:
        for p in grp:
            z_ref[p * rows:(p + 1) * rows, :] = _dot(qbd, _bf(k_refs[p][...])) + bias
    zs, sums = [], []
    for grp in groups:
        zs.append(z_ref[grp[0] * rows:(grp[-1] + 1) * rows, :])
        sums.append(_dot(jnp.concatenate(_split_bf16(_softplus(zs[-1]), 2), axis=1), table))
    c_run = c_ref[...]
    weights = []
    for k in range(len(groups)):
        page_sum = sums[k][:, PAGE_SIZE:]
        nearer = (sum(_dot(later_ref[...], piece) for piece in _split_bf16(page_sum, 2))
                  + jnp.concatenate([c_run] * gs, axis=0))
        weights.append(_bf(jnp.exp(zs[k] - (sums[k][:, :PAGE_SIZE] + nearer))))
        for q in range(gs):
            c_run = c_run + page_sum[q * rows:(q + 1) * rows, :]
    c_ref[...] = c_run
    acc = acc_ref[...]
    for k, grp in enumerate(groups):
        for q, p in enumerate(grp):
            acc = acc + _dot_nt(weights[k][q * rows:(q + 1) * rows, :], _bf(v_refs[p][...]))
    acc_ref[...] = acc

    @pl.when(g == pl.num_programs(1) - 1)
    def _():
        rh = lax.broadcasted_iota(jnp.int32, (rows, SB_W), 0) & (SB_HEADS - 1)
        ch = _div_pow2(lax.broadcasted_iota(jnp.int32, (rows, SB_W), 1), HEAD_DIM)
        own = jnp.where(rh == ch, acc_ref[...], 0.0)
        o_ref[...] = _bf(jnp.sum(own.reshape(n_new, SB_HEADS, SB_W), axis=1))


def _sb_sample(page_table, bias_rows, qbd, knt, vnt, cache_kt, cache_vt, l, n_new):
    db, n_pages = page_table.shape
    pp = SAMPLE_PAGES_PER_STEP
    n_steps = n_pages // pp
    rows = qbd.shape[1]

    def page_spec(i):
        return pl.BlockSpec(
            (None, None, SB_W, PAGE_SIZE),
            lambda b, g, pt: (l, pt[b, (n_steps - 1 - g) * pp + i], 0, 0))

    grid_spec = pltpu.PrefetchScalarGridSpec(
        num_scalar_prefetch=1,
        grid=(db, n_steps),
        in_specs=[
            pl.BlockSpec((rows, LANES), lambda b, g, pt: (0, 0)),
            pl.BlockSpec((None, rows, SB_W), lambda b, g, pt: (b, 0, 0)),
            pl.BlockSpec((None, SB_W, PAGE_SIZE), lambda b, g, pt: (b, 0, 0)),
            pl.BlockSpec((None, SB_W, PAGE_SIZE), lambda b, g, pt: (b, 0, 0)),
        ] + [page_spec(i) for i in range(pp)] * 2,
        out_specs=pl.BlockSpec((None, n_new, SB_W), lambda b, g, pt: (b, 0, 0)),
        scratch_shapes=[
            pltpu.VMEM((rows, LANES), _f32),
            pltpu.VMEM((rows, SB_W), _f32),
            pltpu.VMEM((pp * rows, PAGE_SIZE), _f32),
            pltpu.VMEM((SAMPLE_PAGES_PER_GROUP * rows, SAMPLE_PAGES_PER_GROUP * rows), _bf16),
        ],
    )
    return pl.pallas_call(
        functools.partial(_sbs_kernel, n_new=n_new),
        grid_spec=grid_spec,
        out_shape=jax.ShapeDtypeStruct((db, n_new, SB_W), _bf16),
        compiler_params=_params(("parallel", "arbitrary"), VMEM_LIMIT),
        name="sb_sample",
    )(page_table, bias_rows, qbd, knt, vnt, *([cache_kt] * pp), *([cache_vt] * pp))


def _mixout_kernel(x_ref, gt_ref, npost_ref, og_ref, om_ref, os_ref, w_ref, o_ref):
    y = (_dot(og_ref[...], w_ref[0:GLA_W, :])
         + _dot(om_ref[...], w_ref[GLA_W:GLA_W + GM_W, :])
         + _dot(os_ref[...], w_ref[GLA_W + GM_W:, :]))
    o_ref[...] = x_ref[...] + gt_ref[...] * _rms(y, npost_ref[...])


def _mixout(x, cond, npost, o_gla, o_gm, o_sb, w_out, l, tm):
    m = x.shape[0]
    row = lambda i: (i, 0)
    return pl.pallas_call(
        _mixout_kernel,
        grid=(m // tm,),
        in_specs=[
            pl.BlockSpec((tm, D_MODEL), row),
            cond.spec(1, 2),
            pl.BlockSpec((1, D_MODEL), lambda i: (0, 0)),
            pl.BlockSpec((tm, GLA_W), row),
            pl.BlockSpec((tm, GM_W), row),
            pl.BlockSpec((tm, SB_W), row),
            _resident((None, D_MODEL, D_MODEL), lambda i: (l, 0, 0)),
        ],
        out_specs=pl.BlockSpec((tm, D_MODEL), row),
        out_shape=jax.ShapeDtypeStruct((m, D_MODEL), _f32),
        compiler_params=_params(("parallel",), VMEM_LIMIT),
        name="mixout",
    )(x, cond.arr, npost, o_gla, o_gm, o_sb, w_out)


def kernel(x_prompt, x_sample, c_prompt, c_sample, cache_sb_k, cache_sb_v, state_gla, page_table,
           cond_w, cond_b, norm_pre, norm_post, ffn_w_in, ffn_w_out, mix_w_in, mix_w_out,
           gla_w_gate2, gla_b_gate, gla_norm, gm_ln_w, gm_ln_b, gm_ws, gm_bs, sb_bias):
    depth = cond_w.shape[0]
    bsz, seq, _ = x_prompt.shape
    db, n_new, _ = x_sample.shape
    n_pool = cache_sb_k.shape[1]
    mp, ms = bsz * seq, db * n_new
    assert ms == GM_CHUNK and GM_CHUNK % n_new == 0 and seq % TM_PROMPT == 0

    w_in_b = _bf(ffn_w_in)
    w_out_b = _bf(ffn_w_out)
    mix_out_b = _bf(mix_w_out)
    mix_t = _bf(jnp.swapaxes(mix_w_in, 1, 2))
    w2_pad = _bf(jnp.pad(gla_w_gate2, ((0, 0), (0, LR_PAD - GLA_LOWRANK), (0, 0))))
    to_t = lambda cch: jnp.transpose(cch, (0, 1, 3, 4, 2)).reshape(depth, n_pool, SB_W, PAGE_SIZE)
    cache_kt, cache_vt = to_t(cache_sb_k), to_t(cache_sb_v)
    head_cols = (jnp.arange(SB_W) // HEAD_DIM)[None, :] == jnp.arange(SB_HEADS)[:, None]

    m_all = _cond(jnp.concatenate([c_prompt, c_sample], axis=0), cond_w, cond_b)

    xp = x_prompt.reshape(mp, D_MODEL)
    xs = x_sample.reshape(ms, D_MODEL)
    s0_prompt = jnp.zeros((bsz, GLA_HEADS, HEAD_DIM, HEAD_DIM), _f32)
    gla_p, gla_s, gms = [], [], []
    kv_p, kv_s = [], []
    vec = lambda a: a.reshape(1, -1)
    from_t = lambda t: jnp.transpose(t.reshape(depth, t.shape[1], SB_HEADS, HEAD_DIM, -1), (0, 1, 4, 2, 3))

    for l in range(depth):
        npre = [vec(norm_pre[l, i]) for i in range(3)]
        npost = [vec(norm_post[l, i]) for i in range(3)]
        bg, nw = vec(gla_b_gate[l]), vec(gla_norm[l])
        lnw, lnb = vec(gm_ln_w[l]), vec(gm_ln_b[l])

        cond = _Cond(m_all[l, :bsz], seq, TM_PROMPT)
        xp = _ffn(xp, cond, 0, npre[0], npost[0], w_in_b, w_out_b, l, 0, TM_PROMPT)
        bs_rows = jnp.repeat(gm_bs[l].T, HEAD_DIM, axis=1)
        last = l == depth - 1
        gla, la, o_gm, _, qb, kt, vt, ktb, vtb = _mixin(
            xp, cond, npre[1], mix_t, l, kv_p if last else None, w2_pad[l], bg, lnw, lnb, gm_ws[l], bs_rows,
            TM_PROMPT, GM_CHUNK, seq, SB_TQ)
        o_gla, s_gla = _gla(gla.reshape(bsz, seq, 4 * GLA_W), la.reshape(bsz, seq, GLA_W), nw,
                            s0_prompt, 1, GLA_TL, GLA_CHUNK)
        o_sb = _sb_prompt(qb.reshape(bsz, seq, SB_W), ktb, vtb, sb_bias[l])
        xp = _mixout(xp, cond, npost[1], o_gla.reshape(mp, GLA_W), o_gm, o_sb.reshape(mp, SB_W),
                     mix_out_b, l, TM_PROMPT)
        xp = _ffn(xp, cond, 2, npre[2], npost[2], w_in_b, w_out_b, l, 1, TM_PROMPT)
        gla_p.append(s_gla)
        kv_p += [kt, vt]

        cond = _Cond(m_all[l, bsz:], n_new, ms)
        xs = _ffn(xs, cond, 0, npre[0], npost[0], w_in_b, w_out_b, l, 0, ms)
        reps = GM_CHUNK // n_new
        ws_s = jnp.tile(gm_ws[l][:, :n_new, :n_new], (1, reps, reps))
        bs_s = jnp.tile(bs_rows[:n_new], (reps, 1))
        gla, la, o_gm, vn, qb, kt, vt, ktb, vtb = _mixin(
            xs, cond, npre[1], mix_t, l, kv_s if last else None, w2_pad[l], bg, lnw, lnb, ws_s, bs_s,
            ms, n_new, ms, ms)
        pad_rows = lambda a, n: jnp.pad(a.reshape(db, n_new, -1), ((0, 0), (0, n - n_new), (0, 0)))
        o_gla, s_gla = _gla(pad_rows(gla, SUBLANES), pad_rows(la, SUBLANES), nw, state_gla[l],
                            GLA_SAMPLE_SEQS, SUBLANES, SUBLANES)
        qbd = (qb.reshape(db, n_new, 1, SB_W) * head_cols.astype(_bf16)[None, None]
               ).reshape(db, n_new * SB_HEADS, SB_W)
        bias_rows = jnp.broadcast_to(jnp.tile(sb_bias[l], n_new)[:, None], (n_new * SB_HEADS, LANES))
        new_page = lambda t: jnp.pad(jnp.transpose(t.reshape(SB_W, db, n_new), (1, 0, 2)),
                                     ((0, 0), (0, 0), (0, PAGE_SIZE - n_new)))
        o_sb = _sb_sample(page_table, bias_rows, qbd, new_page(ktb), new_page(vtb),
                          cache_kt, cache_vt, l, n_new)
        xs = _mixout(xs, cond, npost[1], o_gla[:, :n_new].reshape(ms, GLA_W), o_gm,
                     o_sb.reshape(ms, SB_W), mix_out_b, l, ms)
        xs = _ffn(xs, cond, 2, npre[2], npost[2], w_in_b, w_out_b, l, 1, ms)
        gla_s.append(s_gla)
        kv_s += [kt, vt]
        gms.append(vn.reshape(db, n_new, GM_W))

    kp, vp = (from_t(t) for t in kv_p[-2:])
    ksl, vsl = (from_t(t).reshape(depth, db, n_new, SB_HEADS, HEAD_DIM) for t in kv_s[-2:])
    return (xp.reshape(bsz, seq, D_MODEL), xs.reshape(db, n_new, D_MODEL),
            jnp.stack(gla_p), jnp.stack(gla_s), kp, vp, ksl, vsl, jnp.stack(gms))
```

```python
import functools

import jax
import jax.numpy as jnp
from jax import lax
from jax.experimental import pallas as pl
from jax.experimental.pallas import tpu as pltpu

D_MODEL = 1024
HEAD_DIM = 64
GLA_HEADS = 4
GM_HEADS = 4
SB_HEADS = 8
GLA_W = GLA_HEADS * HEAD_DIM
GM_W = GM_HEADS * HEAD_DIM
SB_W = SB_HEADS * HEAD_DIM
GLA_LOWRANK = 16
GLA_TAU = 16.0
GLA_CHUNK = 64
GM_CHUNK = 128
PAGE_SIZE = 128
D_FF = 2816
N_COND = 9
MACARON_W = 0.5
EPS = 1e-6
LOG2E = 1.4426950408889634

LANES = 128
SUBLANES = 8
LR_PAD = LANES
ROW_GLA = 0
ROW_LR = 4 * GLA_W
ROW_GM = ROW_LR + GLA_LOWRANK
ROW_SB = ROW_GM + 2 * GM_W
MIX_ROWS = ROW_SB + 3 * SB_W

COND_TN = 2304
FFN_CHUNK = 256
TM_PROMPT = 512
TM_FFN = 1024
SB_TQ = 256
GLA_TL = 1024
GLA_SAMPLE_SEQS = 4
SAMPLE_PAGES_PER_STEP = 32
SAMPLE_PAGES_PER_GROUP = 4
VMEM_LIMIT = 48 * 1024 * 1024
VMEM_LIMIT_FFN = 56 * 1024 * 1024

_f32 = jnp.float32
_bf16 = jnp.bfloat16


def _bf(x):
    return x.astype(_bf16)


def _dot(a, b):
    return jnp.dot(a, b, preferred_element_type=_f32)


def _dot_nt(a, b):
    return lax.dot_general(a, b, (((1,), (1,)), ((), ())), preferred_element_type=_f32)


def _dot_tn(a, b):
    return lax.dot_general(a, b, (((0,), (0,)), ((), ())), preferred_element_type=_f32)


def _split_bf16(x, passes):
    pieces = []
    rem = x
    for _ in range(passes):
        piece = _bf(rem)
        pieces.append(piece)
        rem = rem - piece.astype(_f32)
    return pieces


def _dot_split(x, m, passes):
    return sum(_dot(p, m) for p in _split_bf16(x, passes))


def _cumsum_rows(x, tri):
    return sum(_dot(tri, p) for p in _split_bf16(x, 3))


def _rms(x, w):
    return x * lax.rsqrt(jnp.mean(x * x, axis=-1, keepdims=True) + EPS) * w


def _sigmoid(x):
    return 1.0 / (1.0 + jnp.exp(-x))


def _log1p_exp_neg_abs(z):
    return jnp.log(1.0 + jnp.exp(-jnp.abs(z)))


def _softplus(z):
    return jnp.maximum(z, 0.0) + jnp.log(1.0 + jnp.exp2(jnp.abs(z) * (-LOG2E)))


def _div_pow2(x, d):
    shift = d.bit_length() - 1
    assert d == 1 << shift
    return lax.shift_right_logical(x, shift)


def _params(sem, vmem=None):
    return pltpu.CompilerParams(dimension_semantics=sem, vmem_limit_bytes=vmem)


def _resident(shape, index_map):
    return pl.BlockSpec(shape, index_map, pipeline_mode=pl.Buffered(1))


def _cond_kernel(c_ref, w_ref, b_ref, o_ref):
    c = c_ref[...]
    s = c * _sigmoid(c)
    o_ref[...] = _dot(_bf(s), _bf(w_ref[...])) + b_ref[...]


def _cond(c_all, cond_w, cond_b):
    depth = cond_w.shape[0]
    n = c_all.shape[0]
    tn = COND_TN
    return pl.pallas_call(
        _cond_kernel,
        grid=(depth, N_COND * D_MODEL // tn),
        in_specs=[
            pl.BlockSpec((n, D_MODEL), lambda l, j: (0, 0)),
            pl.BlockSpec((None, D_MODEL, tn), lambda l, j: (l, 0, j)),
            pl.BlockSpec((None, 1, tn), lambda l, j: (l, 0, j)),
        ],
        out_specs=pl.BlockSpec((None, n, tn), lambda l, j: (l, 0, j)),
        out_shape=jax.ShapeDtypeStruct((depth, n, N_COND * D_MODEL), _f32),
        compiler_params=_params(("parallel", "parallel"), VMEM_LIMIT),
        name="cond",
    )(c_all, cond_w, cond_b.reshape(depth, 1, N_COND * D_MODEL))


class _Cond:
    def __init__(self, m, seq_rows, tm):
        if seq_rows % tm == 0:
            self.arr = m.reshape(m.shape[0], 1, N_COND * D_MODEL)
            per = seq_rows // tm
            self._spec = lambda j: pl.BlockSpec((None, 1, D_MODEL), lambda i: (i // per, 0, j))
        else:
            assert tm % seq_rows == 0
            self.arr = jnp.repeat(m, seq_rows, axis=0)
            self._spec = lambda j: pl.BlockSpec((tm, D_MODEL), lambda i: (i, j))

    def spec(self, sub, kind):
        return self._spec(3 * sub + kind)


def _ffn_kernel(x_ref, sh_ref, sc_ref, gt_ref, npre_ref, npost_ref, win_ref, wout_ref, o_ref,
                a_ref, acc_ref):
    x = x_ref[...]
    a = _rms(x, npre_ref[...]) * (1.0 + sc_ref[...]) + sh_ref[...]
    a_ref[...] = _bf(a)
    for c in range(D_FF // FFN_CHUNK):
        lo = c * FFN_CHUNK
        ab = a_ref[...]
        g = _dot(ab, win_ref[:, lo:lo + FFN_CHUNK])
        u = _dot(ab, win_ref[:, D_FF + lo:D_FF + lo + FFN_CHUNK])
        act = _bf(g * _sigmoid(g) * u)
        y = _dot(act, wout_ref[lo:lo + FFN_CHUNK, :])
        if c == 0:
            acc_ref[...] = y
        else:
            acc_ref[...] += y
    o_ref[...] = x + MACARON_W * gt_ref[...] * _rms(acc_ref[...], npost_ref[...])


def _ffn(x, cond, sub, npre, npost, w_in, w_out, l, idx, tm):
    m = x.shape[0]
    row = lambda i: (i, 0)
    vec = pl.BlockSpec((1, D_MODEL), lambda i: (0, 0))
    return pl.pallas_call(
        _ffn_kernel,
        grid=(m // tm,),
        in_specs=[
            pl.BlockSpec((tm, D_MODEL), row),
            cond.spec(sub, 0), cond.spec(sub, 1), cond.spec(sub, 2),
            vec, vec,
            _resident((None, None, D_MODEL, 2 * D_FF), lambda i: (l, idx, 0, 0)),
            _resident((None, None, D_FF, D_MODEL), lambda i: (l, idx, 0, 0)),
        ],
        out_specs=pl.BlockSpec((tm, D_MODEL), row),
        out_shape=jax.ShapeDtypeStruct((m, D_MODEL), _f32),
        scratch_shapes=[pltpu.VMEM((tm, D_MODEL), _bf16), pltpu.VMEM((tm, D_MODEL), _f32)],
        compiler_params=_params(("parallel",), VMEM_LIMIT_FFN),
        name="ffn",
    )(x, cond.arr, cond.arr, cond.arr, npre, npost, w_in, w_out)


def _mixin_kernel(x_ref, sh_ref, sc_ref, npre_ref, wt_ref, w2_ref, bg_ref, lnw_ref, lnb_ref,
                  ws_ref, bs_ref, *refs, period):
    earlier = refs[:-9]
    gla_ref, la_ref, ogm_ref, vn_ref, qb_ref, kt_ref, vt_ref, ktb_ref, vtb_ref = refs[-9:]
    tm = x_ref.shape[0]
    tk = ktb_ref.shape[-1]
    x = x_ref[...]
    a = _bf(_rms(x, npre_ref[...]) * (1.0 + sc_ref[...]) + sh_ref[...])

    gla_ref[...] = _dot_nt(a, wt_ref[ROW_GLA:ROW_GLA + 4 * GLA_W, :])
    lr = _dot_nt(a, wt_ref[ROW_LR:ROW_LR + LR_PAD, :])
    zl = _dot(_bf(lr), w2_ref[...]) + bg_ref[...]
    la_ref[...] = (jnp.minimum(zl, 0.0) - _log1p_exp_neg_abs(zl)) * (1.0 / GLA_TAU)

    q = _dot_nt(a, wt_ref[ROW_SB:ROW_SB + SB_W, :])
    qb_ref[...] = _bf(q * (HEAD_DIM ** -0.5))
    kt = _dot_nt(wt_ref[ROW_SB + SB_W:ROW_SB + 2 * SB_W, :], a)
    vt = _dot_nt(wt_ref[ROW_SB + 2 * SB_W:ROW_SB + 3 * SB_W, :], a)
    if kt_ref.shape == kt.shape:
        kt_ref[...] = kt
        vt_ref[...] = vt
    else:
        n_earlier = len(earlier) // 2
        kt_ref[n_earlier] = kt
        vt_ref[n_earlier] = vt
        for d in range(n_earlier):
            kt_ref[d] = earlier[2 * d][...]
            vt_ref[d] = earlier[2 * d + 1][...]
    for j in range(tm // tk):
        ktb_ref[j] = _bf(kt[:, j * tk:(j + 1) * tk])
        vtb_ref[j] = _bf(vt[:, j * tk:(j + 1) * tk])

    mu = _dot_nt(a, wt_ref[ROW_GM:ROW_GM + GM_W, :])
    mv = _dot_nt(a, wt_ref[ROW_GM + GM_W:ROW_GM + 2 * GM_W, :])
    mean = jnp.mean(mv, axis=-1, keepdims=True)
    d = mv - mean
    var = jnp.mean(d * d, axis=-1, keepdims=True)
    vn = d * lax.rsqrt(var + EPS) * lnw_ref[...] + lnb_ref[...]
    vn_ref[...] = vn
    vnb = _bf(vn)
    r = lax.broadcasted_iota(jnp.int32, (GM_CHUNK, GM_CHUNK), 0)
    c = lax.broadcasted_iota(jnp.int32, (GM_CHUNK, GM_CHUNK), 1)
    keep = (c <= r) & (_div_pow2(r, period) == _div_pow2(c, period))
    wm = [_bf(jnp.where(keep, ws_ref[h], 0.0)) for h in range(GM_HEADS)]
    col_head = _div_pow2(lax.broadcasted_iota(jnp.int32, (GM_CHUNK, GM_W), 1), HEAD_DIM)
    for ci in range(tm // GM_CHUNK):
        rows = slice(ci * GM_CHUNK, (ci + 1) * GM_CHUNK)
        vc = vnb[rows, :]
        s = bs_ref[...]
        for h in range(GM_HEADS):
            s = s + jnp.where(col_head == h, _dot(wm[h], vc), 0.0)
        ogm_ref[rows, :] = _bf(mu[rows, :] * s)


def _mixin(x, cond, npre, w_t, l, kv_earlier, w2p, bg, lnw, lnb, ws, bs, tm, period, seq, tk):
    m = x.shape[0]
    n_seq, per = m // seq, seq // tm
    extra = [] if kv_earlier is None else list(kv_earlier)
    row = lambda i: (i, 0)
    const2 = lambda i: (0, 0)
    rows_out = [(4 * GLA_W, _f32), (GLA_W, _f32), (GM_W, _bf16), (GM_W, _f32), (SB_W, _bf16)]
    t_in = pl.BlockSpec((None, SB_W, tm), lambda i: (i // per, 0, i % per))
    if kv_earlier is None:
        t_spec, t_shape = t_in, jax.ShapeDtypeStruct((n_seq, SB_W, seq), _f32)
    else:
        depth = len(extra) // 2 + 1
        t_spec = pl.BlockSpec((depth, None, SB_W, tm), lambda i: (0, i // per, 0, i % per))
        t_shape = jax.ShapeDtypeStruct((depth, n_seq, SB_W, seq), _f32)
    tb_spec = pl.BlockSpec((None, tm // tk, SB_W, tk), lambda i: (i // per, i % per, 0, 0))
    tb_shape = jax.ShapeDtypeStruct((n_seq, seq // tk, SB_W, tk), _bf16)
    return pl.pallas_call(
        functools.partial(_mixin_kernel, period=period),
        grid=(m // tm,),
        in_specs=[
            pl.BlockSpec((tm, D_MODEL), row),
            cond.spec(1, 0), cond.spec(1, 1),
            pl.BlockSpec((1, D_MODEL), const2),
            _resident((None, MIX_ROWS, D_MODEL), lambda i: (l, 0, 0)),
            pl.BlockSpec((LR_PAD, GLA_W), const2),
            pl.BlockSpec((1, GLA_W), const2),
            pl.BlockSpec((1, GM_W), const2),
            pl.BlockSpec((1, GM_W), const2),
            pl.BlockSpec((GM_HEADS, GM_CHUNK, GM_CHUNK), lambda i: (0, 0, 0)),
            pl.BlockSpec((GM_CHUNK, GM_W), const2),
        ] + [t_in] * len(extra),
        out_specs=[pl.BlockSpec((tm, w), row) for w, _ in rows_out] + [t_spec, t_spec, tb_spec, tb_spec],
        out_shape=[jax.ShapeDtypeStruct((m, w), dt) for w, dt in rows_out]
        + [t_shape, t_shape, tb_shape, tb_shape],
        compiler_params=_params(("parallel",), VMEM_LIMIT),
        name="mixin",
    )(x, cond.arr, cond.arr, npre, w_t, w2p, bg, lnw, lnb, ws, bs, *extra)


def _gla_kernel(gla_ref, la_ref, nw_ref, s0_ref, o_ref, s_ref, st_ref, ob_ref, *, chunk):
    nb, tl = gla_ref.shape[:2]
    j = pl.program_id(1)
    heads = [slice(h * HEAD_DIM, (h + 1) * HEAD_DIM) for h in range(GLA_HEADS)]

    @pl.when(j == 0)
    def _():
        for s in range(nb):
            for h in range(GLA_HEADS):
                st_ref[s, h] = s0_ref[s, h].T

    ri = lax.broadcasted_iota(jnp.int32, (chunk, chunk), 0)
    ci = lax.broadcasted_iota(jnp.int32, (chunk, chunk), 1)
    causal = ci <= ri
    tri = _bf(jnp.where(causal, 1.0, 0.0))
    gi = _div_pow2(lax.broadcasted_iota(jnp.int32, (GLA_W, GLA_W), 0), HEAD_DIM)
    gj = _div_pow2(lax.broadcasted_iota(jnp.int32, (GLA_W, GLA_W), 1), HEAD_DIM)
    group = _bf(jnp.where(gi == gj, 1.0, 0.0))

    n = tl // chunk
    units = [(s, slice(c * chunk, (c + 1) * chunk)) for s in range(nb) for c in range(n)]
    b = [_cumsum_rows(la_ref[s, r, :], tri) for s, r in units]
    q_t, k_t, k_end, decay, v = [], [], [], [], []
    for u, (s, r) in enumerate(units):
        k = gla_ref[s, r, GLA_W:2 * GLA_W]
        b_last = b[u][chunk - 1:chunk, :]
        q_t.append(_bf(gla_ref[s, r, 0:GLA_W] * (HEAD_DIM ** -0.5) * jnp.exp(b[u])))
        k_t.append(_bf(k * jnp.exp(-b[u])))
        k_end.append(_bf(k * jnp.exp(b_last - b[u])))
        decay.append(jnp.exp(b_last))
        v.append(_bf(gla_ref[s, r, 2 * GLA_W:3 * GLA_W]))
    att = [[_bf(jnp.where(causal, _dot_nt(q_t[u][:, hs], k_t[u][:, hs]), 0.0)) for hs in heads]
           for u in range(len(units))]
    kv = [[_dot_tn(v[u][:, hs], k_end[u][:, hs]) for hs in heads] for u in range(len(units))]
    for u, (s, r) in enumerate(units):
        for h, hs in enumerate(heads):
            ob_ref[s, r, hs] = _dot(att[u][h], v[u][:, hs])
    entry = {}
    for s in range(nb):
        for h, hs in enumerate(heads):
            st = st_ref[s, h]
            for c in range(n):
                u = s * n + c
                entry[u, h] = _bf(st)
                st = st * decay[u][:, hs] + kv[u][h]
            st_ref[s, h] = st
    for u, (s, r) in enumerate(units):
        for h, hs in enumerate(heads):
            ob_ref[s, r, hs] += _dot_nt(q_t[u][:, hs], entry[u, h])
    for s, r in units:
        o = ob_ref[s, r, :]
        g = gla_ref[s, r, 3 * GLA_W:4 * GLA_W]
        ms = _dot_split(o * o, group, 2) * (1.0 / HEAD_DIM)
        y = o * lax.rsqrt(ms + EPS) * nw_ref[...]
        o_ref[s, r, :] = _bf(y * (g * _sigmoid(g)))

    @pl.when(j == pl.num_programs(1) - 1)
    def _():
        for s in range(nb):
            for h in range(GLA_HEADS):
                s_ref[s, h] = st_ref[s, h].T


def _gla(gla, la, nw, s0, nb, tl, chunk):
    bsz, length, _ = gla.shape
    state_spec = pl.BlockSpec((nb, GLA_HEADS, HEAD_DIM, HEAD_DIM), lambda b, j: (b, 0, 0, 0))
    return pl.pallas_call(
        functools.partial(_gla_kernel, chunk=chunk),
        grid=(bsz // nb, length // tl),
        in_specs=[
            pl.BlockSpec((nb, tl, 4 * GLA_W), lambda b, j: (b, j, 0)),
            pl.BlockSpec((nb, tl, GLA_W), lambda b, j: (b, j, 0)),
            pl.BlockSpec((1, GLA_W), lambda b, j: (0, 0)),
            state_spec,
        ],
        out_specs=[pl.BlockSpec((nb, tl, GLA_W), lambda b, j: (b, j, 0)), state_spec],
        out_shape=[
            jax.ShapeDtypeStruct((bsz, length, GLA_W), _bf16),
            jax.ShapeDtypeStruct((bsz, GLA_HEADS, HEAD_DIM, HEAD_DIM), _f32),
        ],
        scratch_shapes=[
            pltpu.VMEM((nb, GLA_HEADS, HEAD_DIM, HEAD_DIM), _f32),
            pltpu.VMEM((nb, tl, GLA_W), _f32),
        ],
        compiler_params=_params(("parallel", "arbitrary")),
        name="gla",
    )(gla, la, nw, s0)


def _sb_table(tk, with_total):
    cols = tk + LANES if with_total else tk
    r = lax.broadcasted_iota(jnp.int32, (2 * tk, cols), 0) & (tk - 1)
    c = lax.broadcasted_iota(jnp.int32, (2 * tk, cols), 1)
    return _bf(jnp.where((r >= c) | (c >= tk), 1.0, 0.0))


def _sb_weights(z, table, c_run, mask):
    tk = z.shape[1]
    sp = _softplus(z)
    if mask is not None:
        sp = jnp.where(mask, sp, 0.0)
    sums = _dot(jnp.concatenate(_split_bf16(sp, 2), axis=1), table)
    a = jnp.exp(z - (sums[:, :tk] + c_run))
    if mask is not None:
        a = jnp.where(mask, a, 0.0)
    return _bf(a), sums


def _sbp_kernel(bias_ref, q_ref, k_ref, v_ref, o_ref,
                table_ref, qh_ref, c_ref, acc_ref, z_ref, sp_ref, a_ref):
    tq = q_ref.shape[0]
    assert k_ref.shape[-1] == tq
    i = pl.program_id(1)
    table_ref[...] = _sb_table(tq, with_total=False)
    heads = [slice(h * HEAD_DIM, (h + 1) * HEAD_DIM) for h in range(SB_HEADS)]
    for h, hs in enumerate(heads):
        qh_ref[h] = q_ref[:, hs]

    def visit(j, mask, first):
        for h, hs in enumerate(heads):
            z = _dot(qh_ref[h], k_ref[j, hs, :]) + bias_ref[h]
            sp = _softplus(z)
            if mask is not None:
                sp = jnp.where(mask, sp, 0.0)
            z_ref[h] = z
            sp_ref[h] = jnp.concatenate(_split_bf16(sp, 2), axis=1)
        for h in range(SB_HEADS):
            tail = _dot(sp_ref[h], table_ref[...])
            c_run = jnp.zeros((tq, LANES), _f32) if first else c_ref[h]
            a = jnp.exp(z_ref[h] - (tail + jnp.concatenate([c_run] * (tq // LANES), axis=1)))
            if mask is not None:
                a = jnp.where(mask, a, 0.0)
            a_ref[h] = _bf(a)
            block_sum = jnp.broadcast_to(tail[:, 0:1], (tq, LANES))
            c_ref[h] = block_sum if first else c_run + block_sum
        for h, hs in enumerate(heads):
            av = _dot_nt(a_ref[h], v_ref[j, hs, :])
            acc_ref[h] = av if first else acc_ref[h] + av

    r = lax.broadcasted_iota(jnp.int32, (tq, tq), 0)
    c = lax.broadcasted_iota(jnp.int32, (tq, tq), 1)
    visit(i, c < r, True)

    def body(jj, carry):
        visit(i - 1 - jj, None, False)
        return carry

    lax.fori_loop(0, i, body, 0)
    for h, hs in enumerate(heads):
        o_ref[:, hs] = _bf(acc_ref[h])


def _sb_prompt(qb, ktb, vtb, bias):
    bsz, length, _ = qb.shape
    nblk, _, tq = ktb.shape[1:]
    kv_spec = pl.BlockSpec((None, nblk, SB_W, tq), lambda b, i: (b, 0, 0, 0))
    return pl.pallas_call(
        _sbp_kernel,
        grid=(bsz, nblk),
        in_specs=[
            pl.BlockSpec(memory_space=pltpu.SMEM),
            pl.BlockSpec((None, tq, SB_W), lambda b, i: (b, i, 0)),
            kv_spec, kv_spec,
        ],
        out_specs=pl.BlockSpec((None, tq, SB_W), lambda b, i: (b, i, 0)),
        out_shape=jax.ShapeDtypeStruct((bsz, length, SB_W), _bf16),
        scratch_shapes=[
            pltpu.VMEM((2 * tq, tq), _bf16),
            pltpu.VMEM((SB_HEADS, tq, HEAD_DIM), _bf16),
            pltpu.VMEM((SB_HEADS, tq, LANES), _f32),
            pltpu.VMEM((SB_HEADS, tq, HEAD_DIM), _f32),
            pltpu.VMEM((SB_HEADS, tq, tq), _f32),
            pltpu.VMEM((SB_HEADS, tq, 2 * tq), _bf16),
            pltpu.VMEM((SB_HEADS, tq, tq), _bf16),
        ],
        compiler_params=_params(("parallel", "arbitrary"), VMEM_LIMIT),
        name="sb_prompt",
    )(bias, qb, ktb, vtb)


def _sbs_kernel(pt_ref, bias_ref, qbd_ref, kn_ref, vn_ref, *refs, n_new):
    pp, gs = SAMPLE_PAGES_PER_STEP, SAMPLE_PAGES_PER_GROUP
    k_refs, v_refs = refs[:pp], refs[pp:2 * pp]
    o_ref, c_ref, acc_ref, z_ref, later_ref = refs[2 * pp:]
    g = pl.program_id(1)
    rows = qbd_ref.shape[0]
    n = gs * rows
    table = _sb_table(PAGE_SIZE, with_total=True)
    qbd = qbd_ref[...]
    bias = bias_ref[...]

    @pl.when(g == 0)
    def _():
        ri = lax.broadcasted_iota(jnp.int32, (n, n), 0)
        ci = lax.broadcasted_iota(jnp.int32, (n, n), 1)
        same_row = (ri & (rows - 1)) == (ci & (rows - 1))
        later_ref[...] = _bf(jnp.where(same_row & (_div_pow2(ci, rows) > _div_pow2(ri, rows)), 1.0, 0.0))
        t = _div_pow2(lax.broadcasted_iota(jnp.int32, (rows, PAGE_SIZE), 0), SB_HEADS)
        s = lax.broadcasted_iota(jnp.int32, (rows, PAGE_SIZE), 1)
        a, sums = _sb_weights(_dot(qbd, kn_ref[...]) + bias, table,
                                 jnp.zeros((rows, PAGE_SIZE), _f32), s < t)
        acc_ref[...] = _dot_nt(a, vn_ref[...])
        c_ref[...] = sums[:, PAGE_SIZE:]

    groups = [range(k * gs, (k + 1) * gs) for k in reversed(range(pp // gs))]
    for grp in groups:
        for p in grp:
            z_ref[p * rows:(p + 1) * rows, :] = _dot(qbd, _bf(k_refs[p][...])) + bias
    zs, sums = [], []
    for grp in groups:
        zs.append(z_ref[grp[0] * rows:(grp[-1] + 1) * rows, :])
        sums.append(_dot(jnp.concatenate(_split_bf16(_softplus(zs[-1]), 2), axis=1), table))
    c_run = c_ref[...]
    weights = []
    for k in range(len(groups)):
        page_sum = sums[k][:, PAGE_SIZE:]
        nearer = (sum(_dot(later_ref[...], piece) for piece in _split_bf16(page_sum, 2))
                  + jnp.concatenate([c_run] * gs, axis=0))
        weights.append(_bf(jnp.exp(zs[k] - (sums[k][:, :PAGE_SIZE] + nearer))))
        for q in range(gs):
            c_run = c_run + page_sum[q * rows:(q + 1) * rows, :]
    c_ref[...] = c_run
    acc = acc_ref[...]
    for k, grp in enumerate(groups):
        for q, p in enumerate(grp):
            acc = acc + _dot_nt(weights[k][q * rows:(q + 1) * rows, :], _bf(v_refs[p][...]))
    acc_ref[...] = acc

    @pl.when(g == pl.num_programs(1) - 1)
    def _():
        rh = lax.broadcasted_iota(jnp.int32, (rows, SB_W), 0) & (SB_HEADS - 1)
        ch = _div_pow2(lax.broadcasted_iota(jnp.int32, (rows, SB_W), 1), HEAD_DIM)
        own = jnp.where(rh == ch, acc_ref[...], 0.0)
        o_ref[...] = _bf(jnp.sum(own.reshape(n_new, SB_HEADS, SB_W), axis=1))


def _sb_sample(page_table, bias_rows, qbd, knt, vnt, cache_kt, cache_vt, l, n_new):
    db, n_pages = page_table.shape
    pp = SAMPLE_PAGES_PER_STEP
    n_steps = n_pages // pp
    rows = qbd.shape[1]

    def page_spec(i):
        return pl.BlockSpec(
            (None, None, SB_W, PAGE_SIZE),
            lambda b, g, pt: (l, pt[b, (n_steps - 1 - g) * pp + i], 0, 0))

    grid_spec = pltpu.PrefetchScalarGridSpec(
        num_scalar_prefetch=1,
        grid=(db, n_steps),
        in_specs=[
            pl.BlockSpec((rows, LANES), lambda b, g, pt: (0, 0)),
            pl.BlockSpec((None, rows, SB_W), lambda b, g, pt: (b, 0, 0)),
            pl.BlockSpec((None, SB_W, PAGE_SIZE), lambda b, g, pt: (b, 0, 0)),
            pl.BlockSpec((None, SB_W, PAGE_SIZE), lambda b, g, pt: (b, 0, 0)),
        ] + [page_spec(i) for i in range(pp)] * 2,
        out_specs=pl.BlockSpec((None, n_new, SB_W), lambda b, g, pt: (b, 0, 0)),
        scratch_shapes=[
            pltpu.VMEM((rows, LANES), _f32),
            pltpu.VMEM((rows, SB_W), _f32),
            pltpu.VMEM((pp * rows, PAGE_SIZE), _f32),
            pltpu.VMEM((SAMPLE_PAGES_PER_GROUP * rows, SAMPLE_PAGES_PER_GROUP * rows), _bf16),
        ],
    )
    return pl.pallas_call(
        functools.partial(_sbs_kernel, n_new=n_new),
        grid_spec=grid_spec,
        out_shape=jax.ShapeDtypeStruct((db, n_new, SB_W), _bf16),
        compiler_params=_params(("parallel", "arbitrary"), VMEM_LIMIT),
        name="sb_sample",
    )(page_table, bias_rows, qbd, knt, vnt, *([cache_kt] * pp), *([cache_vt] * pp))


def _mixout_kernel(x_ref, gt_ref, npost_ref, og_ref, om_ref, os_ref, w_ref, o_ref):
    y = (_dot(og_ref[...], w_ref[0:GLA_W, :])
         + _dot(om_ref[...], w_ref[GLA_W:GLA_W + GM_W, :])
         + _dot(os_ref[...], w_ref[GLA_W + GM_W:, :]))
    o_ref[...] = x_ref[...] + gt_ref[...] * _rms(y, npost_ref[...])


def _mixout(x, cond, npost, o_gla, o_gm, o_sb, w_out, l, tm):
    m = x.shape[0]
    row = lambda i: (i, 0)
    return pl.pallas_call(
        _mixout_kernel,
        grid=(m // tm,),
        in_specs=[
            pl.BlockSpec((tm, D_MODEL), row),
            cond.spec(1, 2),
            pl.BlockSpec((1, D_MODEL), lambda i: (0, 0)),
            pl.BlockSpec((tm, GLA_W), row),
            pl.BlockSpec((tm, GM_W), row),
            pl.BlockSpec((tm, SB_W), row),
            _resident((None, D_MODEL, D_MODEL), lambda i: (l, 0, 0)),
        ],
        out_specs=pl.BlockSpec((tm, D_MODEL), row),
        out_shape=jax.ShapeDtypeStruct((m, D_MODEL), _f32),
        compiler_params=_params(("parallel",), VMEM_LIMIT),
        name="mixout",
    )(x, cond.arr, npost, o_gla, o_gm, o_sb, w_out)


def kernel(x_prompt, x_sample, c_prompt, c_sample, cache_sb_k, cache_sb_v, state_gla, page_table,
           cond_w, cond_b, norm_pre, norm_post, ffn_w_in, ffn_w_out, mix_w_in, mix_w_out,
           gla_w_gate2, gla_b_gate, gla_norm, gm_ln_w, gm_ln_b, gm_ws, gm_bs, sb_bias):
    depth = cond_w.shape[0]
    bsz, seq, _ = x_prompt.shape
    db, n_new, _ = x_sample.shape
    n_pool = cache_sb_k.shape[1]
    mp, ms = bsz * seq, db * n_new
    assert ms == GM_CHUNK and GM_CHUNK % n_new == 0 and seq % TM_PROMPT == 0 and seq % TM_FFN == 0

    w_in_b = _bf(ffn_w_in)
    w_out_b = _bf(ffn_w_out)
    mix_out_b = _bf(mix_w_out)
    mix_t = _bf(jnp.swapaxes(mix_w_in, 1, 2))
    w2_pad = _bf(jnp.pad(gla_w_gate2, ((0, 0), (0, LR_PAD - GLA_LOWRANK), (0, 0))))
    to_t = lambda cch: jnp.transpose(cch, (0, 1, 3, 4, 2)).reshape(depth, n_pool, SB_W, PAGE_SIZE)
    cache_kt, cache_vt = to_t(cache_sb_k), to_t(cache_sb_v)
    head_cols = (jnp.arange(SB_W) // HEAD_DIM)[None, :] == jnp.arange(SB_HEADS)[:, None]

    m_all = _cond(jnp.concatenate([c_prompt, c_sample], axis=0), cond_w, cond_b)

    xp = x_prompt.reshape(mp, D_MODEL)
    xs = x_sample.reshape(ms, D_MODEL)
    s0_prompt = jnp.zeros((bsz, GLA_HEADS, HEAD_DIM, HEAD_DIM), _f32)
    gla_p, gla_s, gms = [], [], []
    kv_p, kv_s = [], []
    vec = lambda a: a.reshape(1, -1)
    from_t = lambda t: jnp.transpose(t.reshape(depth, t.shape[1], SB_HEADS, HEAD_DIM, -1), (0, 1, 4, 2, 3))

    for l in range(depth):
        npre = [vec(norm_pre[l, i]) for i in range(3)]
        npost = [vec(norm_post[l, i]) for i in range(3)]
        bg, nw = vec(gla_b_gate[l]), vec(gla_norm[l])
        lnw, lnb = vec(gm_ln_w[l]), vec(gm_ln_b[l])

        cond = _Cond(m_all[l, :bsz], seq, TM_PROMPT)
        cond_ffn = _Cond(m_all[l, :bsz], seq, TM_FFN)
        xp = _ffn(xp, cond_ffn, 0, npre[0], npost[0], w_in_b, w_out_b, l, 0, TM_FFN)
        bs_rows = jnp.repeat(gm_bs[l].T, HEAD_DIM, axis=1)
        last = l == depth - 1
        gla, la, o_gm, _, qb, kt, vt, ktb, vtb = _mixin(
            xp, cond, npre[1], mix_t, l, kv_p if last else None, w2_pad[l], bg, lnw, lnb, gm_ws[l], bs_rows,
            TM_PROMPT, GM_CHUNK, seq, SB_TQ)
        o_gla, s_gla = _gla(gla.reshape(bsz, seq, 4 * GLA_W), la.reshape(bsz, seq, GLA_W), nw,
                            s0_prompt, 1, GLA_TL, GLA_CHUNK)
        o_sb = _sb_prompt(qb.reshape(bsz, seq, SB_W), ktb, vtb, sb_bias[l])
        xp = _mixout(xp, cond, npost[1], o_gla.reshape(mp, GLA_W), o_gm, o_sb.reshape(mp, SB_W),
                     mix_out_b, l, TM_PROMPT)
        xp = _ffn(xp, cond_ffn, 2, npre[2], npost[2], w_in_b, w_out_b, l, 1, TM_FFN)
        gla_p.append(s_gla)
        kv_p += [kt, vt]

        cond = _Cond(m_all[l, bsz:], n_new, ms)
        xs = _ffn(xs, cond, 0, npre[0], npost[0], w_in_b, w_out_b, l, 0, ms)
        reps = GM_CHUNK // n_new
        ws_s = jnp.tile(gm_ws[l][:, :n_new, :n_new], (1, reps, reps))
        bs_s = jnp.tile(bs_rows[:n_new], (reps, 1))
        gla, la, o_gm, vn, qb, kt, vt, ktb, vtb = _mixin(
            xs, cond, npre[1], mix_t, l, kv_s if last else None, w2_pad[l], bg, lnw, lnb, ws_s, bs_s,
            ms, n_new, ms, ms)
        pad_rows = lambda a, n: jnp.pad(a.reshape(db, n_new, -1), ((0, 0), (0, n - n_new), (0, 0)))
        o_gla, s_gla = _gla(pad_rows(gla, SUBLANES), pad_rows(la, SUBLANES), nw, state_gla[l],
                            GLA_SAMPLE_SEQS, SUBLANES, SUBLANES)
        qbd = (qb.reshape(db, n_new, 1, SB_W) * head_cols.astype(_bf16)[None, None]
               ).reshape(db, n_new * SB_HEADS, SB_W)
        bias_rows = jnp.broadcast_to(jnp.tile(sb_bias[l], n_new)[:, None], (n_new * SB_HEADS, LANES))
        new_page = lambda t: jnp.pad(jnp.transpose(t.reshape(SB_W, db, n_new), (1, 0, 2)),
                                     ((0, 0), (0, 0), (0, PAGE_SIZE - n_new)))
        o_sb = _sb_sample(page_table, bias_rows, qbd, new_page(ktb), new_page(vtb),
                          cache_kt, cache_vt, l, n_new)
        xs = _mixout(xs, cond, npost[1], o_gla[:, :n_new].reshape(ms, GLA_W), o_gm,
                     o_sb.reshape(ms, SB_W), mix_out_b, l, ms)
        xs = _ffn(xs, cond, 2, npre[2], npost[2], w_in_b, w_out_b, l, 1, ms)
        gla_s.append(s_gla)
        kv_s += [kt, vt]
        gms.append(vn.reshape(db, n_new, GM_W))

    kp, vp = (from_t(t) for t in kv_p[-2:])
    ksl, vsl = (from_t(t).reshape(depth, db, n_new, SB_HEADS, HEAD_DIM) for t in kv_s[-2:])
    return (xp.reshape(bsz, seq, D_MODEL), xs.reshape(db, n_new, D_MODEL),
            jnp.stack(gla_p), jnp.stack(gla_s), kp, vp, ksl, vsl, jnp.stack(gms))
```

```python
import functools

import jax
import jax.numpy as jnp
from jax import lax
from jax.experimental import pallas as pl
from jax.experimental.pallas import tpu as pltpu

D_MODEL = 1024
HEAD_DIM = 64
GLA_HEADS = 4
GM_HEADS = 4
SB_HEADS = 8
GLA_W = GLA_HEADS * HEAD_DIM
GM_W = GM_HEADS * HEAD_DIM
SB_W = SB_HEADS * HEAD_DIM
GLA_LOWRANK = 16
GLA_TAU = 16.0
GLA_CHUNK = 64
GM_CHUNK = 128
PAGE_SIZE = 128
D_FF = 2816
N_COND = 9
MACARON_W = 0.5
EPS = 1e-6
LOG2E = 1.4426950408889634

LANES = 128
SUBLANES = 8
LR_PAD = LANES
ROW_GLA = 0
ROW_LR = 4 * GLA_W
ROW_GM = ROW_LR + GLA_LOWRANK
ROW_SB = ROW_GM + 2 * GM_W
MIX_ROWS = ROW_SB + 3 * SB_W

COND_TN = 2304
FFN_CHUNK = 256
TM_PROMPT = 512
TM_FFN = 1024
SB_TQ = 256
GLA_TL = 1024
GLA_SAMPLE_SEQS = 4
SAMPLE_PAGES_PER_STEP = 16
SAMPLE_PAGE_BUFFERS = 3
SAMPLE_PAGES_PER_GROUP = 4
VMEM_LIMIT = 48 * 1024 * 1024
VMEM_LIMIT_FFN = 56 * 1024 * 1024

_f32 = jnp.float32
_bf16 = jnp.bfloat16


def _bf(x):
    return x.astype(_bf16)


def _dot(a, b):
    return jnp.dot(a, b, preferred_element_type=_f32)


def _dot_nt(a, b):
    return lax.dot_general(a, b, (((1,), (1,)), ((), ())), preferred_element_type=_f32)


def _dot_tn(a, b):
    return lax.dot_general(a, b, (((0,), (0,)), ((), ())), preferred_element_type=_f32)


def _split_bf16(x, passes):
    pieces = []
    rem = x
    for _ in range(passes):
        piece = _bf(rem)
        pieces.append(piece)
        rem = rem - piece.astype(_f32)
    return pieces


def _dot_split(x, m, passes):
    return sum(_dot(p, m) for p in _split_bf16(x, passes))


def _cumsum_rows(x, tri):
    return sum(_dot(tri, p) for p in _split_bf16(x, 3))


def _rms(x, w):
    return x * lax.rsqrt(jnp.mean(x * x, axis=-1, keepdims=True) + EPS) * w


def _sigmoid(x):
    return 1.0 / (1.0 + jnp.exp(-x))


def _log1p_exp_neg_abs(z):
    return jnp.log(1.0 + jnp.exp(-jnp.abs(z)))


def _softplus(z):
    return jnp.maximum(z, 0.0) + jnp.log(1.0 + jnp.exp2(jnp.abs(z) * (-LOG2E)))


def _div_pow2(x, d):
    shift = d.bit_length() - 1
    assert d == 1 << shift
    return lax.shift_right_logical(x, shift)


def _params(sem, vmem=None):
    return pltpu.CompilerParams(dimension_semantics=sem, vmem_limit_bytes=vmem)


def _resident(shape, index_map):
    return pl.BlockSpec(shape, index_map, pipeline_mode=pl.Buffered(1))


def _cond_kernel(c_ref, w_ref, b_ref, o_ref):
    c = c_ref[...]
    s = c * _sigmoid(c)
    o_ref[...] = _dot(_bf(s), _bf(w_ref[...])) + b_ref[...]


def _cond(c_all, cond_w, cond_b):
    depth = cond_w.shape[0]
    n = c_all.shape[0]
    tn = COND_TN
    return pl.pallas_call(
        _cond_kernel,
        grid=(depth, N_COND * D_MODEL // tn),
        in_specs=[
            pl.BlockSpec((n, D_MODEL), lambda l, j: (0, 0)),
            pl.BlockSpec((None, D_MODEL, tn), lambda l, j: (l, 0, j)),
            pl.BlockSpec((None, 1, tn), lambda l, j: (l, 0, j)),
        ],
        out_specs=pl.BlockSpec((None, n, tn), lambda l, j: (l, 0, j)),
        out_shape=jax.ShapeDtypeStruct((depth, n, N_COND * D_MODEL), _f32),
        compiler_params=_params(("parallel", "parallel"), VMEM_LIMIT),
        name="cond",
    )(c_all, cond_w, cond_b.reshape(depth, 1, N_COND * D_MODEL))


class _Cond:
    def __init__(self, m, seq_rows, tm):
        if seq_rows % tm == 0:
            self.arr = m.reshape(m.shape[0], 1, N_COND * D_MODEL)
            per = seq_rows // tm
            self._spec = lambda j: pl.BlockSpec((None, 1, D_MODEL), lambda i: (i // per, 0, j))
        else:
            assert tm % seq_rows == 0
            self.arr = jnp.repeat(m, seq_rows, axis=0)
            self._spec = lambda j: pl.BlockSpec((tm, D_MODEL), lambda i: (i, j))

    def spec(self, sub, kind):
        return self._spec(3 * sub + kind)


def _ffn_kernel(x_ref, sh_ref, sc_ref, gt_ref, npre_ref, npost_ref, win_ref, wout_ref, o_ref,
                a_ref, acc_ref):
    x = x_ref[...]
    a = _rms(x, npre_ref[...]) * (1.0 + sc_ref[...]) + sh_ref[...]
    a_ref[...] = _bf(a)
    for c in range(D_FF // FFN_CHUNK):
        lo = c * FFN_CHUNK
        ab = a_ref[...]
        g = _dot(ab, win_ref[:, lo:lo + FFN_CHUNK])
        u = _dot(ab, win_ref[:, D_FF + lo:D_FF + lo + FFN_CHUNK])
        act = _bf(g * _sigmoid(g) * u)
        y = _dot(act, wout_ref[lo:lo + FFN_CHUNK, :])
        if c == 0:
            acc_ref[...] = y
        else:
            acc_ref[...] += y
    o_ref[...] = x + MACARON_W * gt_ref[...] * _rms(acc_ref[...], npost_ref[...])


def _ffn(x, cond, sub, npre, npost, w_in, w_out, l, idx, tm):
    m = x.shape[0]
    row = lambda i: (i, 0)
    vec = pl.BlockSpec((1, D_MODEL), lambda i: (0, 0))
    return pl.pallas_call(
        _ffn_kernel,
        grid=(m // tm,),
        in_specs=[
            pl.BlockSpec((tm, D_MODEL), row),
            cond.spec(sub, 0), cond.spec(sub, 1), cond.spec(sub, 2),
            vec, vec,
            _resident((None, None, D_MODEL, 2 * D_FF), lambda i: (l, idx, 0, 0)),
            _resident((None, None, D_FF, D_MODEL), lambda i: (l, idx, 0, 0)),
        ],
        out_specs=pl.BlockSpec((tm, D_MODEL), row),
        out_shape=jax.ShapeDtypeStruct((m, D_MODEL), _f32),
        scratch_shapes=[pltpu.VMEM((tm, D_MODEL), _bf16), pltpu.VMEM((tm, D_MODEL), _f32)],
        compiler_params=_params(("parallel",), VMEM_LIMIT_FFN),
        name="ffn",
    )(x, cond.arr, cond.arr, cond.arr, npre, npost, w_in, w_out)


def _mixin_kernel(x_ref, sh_ref, sc_ref, npre_ref, wt_ref, w2_ref, bg_ref, lnw_ref, lnb_ref,
                  ws_ref, bs_ref, *refs, period):
    earlier = refs[:-9]
    gla_ref, la_ref, ogm_ref, vn_ref, qb_ref, kt_ref, vt_ref, ktb_ref, vtb_ref = refs[-9:]
    tm = x_ref.shape[0]
    tk = ktb_ref.shape[-1]
    x = x_ref[...]
    a = _bf(_rms(x, npre_ref[...]) * (1.0 + sc_ref[...]) + sh_ref[...])

    gla_ref[...] = _dot_nt(a, wt_ref[ROW_GLA:ROW_GLA + 4 * GLA_W, :])
    lr = _dot_nt(a, wt_ref[ROW_LR:ROW_LR + LR_PAD, :])
    zl = _dot(_bf(lr), w2_ref[...]) + bg_ref[...]
    la_ref[...] = (jnp.minimum(zl, 0.0) - _log1p_exp_neg_abs(zl)) * (1.0 / GLA_TAU)

    q = _dot_nt(a, wt_ref[ROW_SB:ROW_SB + SB_W, :])
    qb_ref[...] = _bf(q * (HEAD_DIM ** -0.5))
    kt = _dot_nt(wt_ref[ROW_SB + SB_W:ROW_SB + 2 * SB_W, :], a)
    vt = _dot_nt(wt_ref[ROW_SB + 2 * SB_W:ROW_SB + 3 * SB_W, :], a)
    if kt_ref.shape == kt.shape:
        kt_ref[...] = kt
        vt_ref[...] = vt
    else:
        n_earlier = len(earlier) // 2
        kt_ref[n_earlier] = kt
        vt_ref[n_earlier] = vt
        for d in range(n_earlier):
            kt_ref[d] = earlier[2 * d][...]
            vt_ref[d] = earlier[2 * d + 1][...]
    for j in range(tm // tk):
        ktb_ref[j] = _bf(kt[:, j * tk:(j + 1) * tk])
        vtb_ref[j] = _bf(vt[:, j * tk:(j + 1) * tk])

    mu = _dot_nt(a, wt_ref[ROW_GM:ROW_GM + GM_W, :])
    mv = _dot_nt(a, wt_ref[ROW_GM + GM_W:ROW_GM + 2 * GM_W, :])
    mean = jnp.mean(mv, axis=-1, keepdims=True)
    d = mv - mean
    var = jnp.mean(d * d, axis=-1, keepdims=True)
    vn = d * lax.rsqrt(var + EPS) * lnw_ref[...] + lnb_ref[...]
    vn_ref[...] = vn
    vnb = _bf(vn)
    r = lax.broadcasted_iota(jnp.int32, (GM_CHUNK, GM_CHUNK), 0)
    c = lax.broadcasted_iota(jnp.int32, (GM_CHUNK, GM_CHUNK), 1)
    keep = (c <= r) & (_div_pow2(r, period) == _div_pow2(c, period))
    wm = [_bf(jnp.where(keep, ws_ref[h], 0.0)) for h in range(GM_HEADS)]
    col_head = _div_pow2(lax.broadcasted_iota(jnp.int32, (GM_CHUNK, GM_W), 1), HEAD_DIM)
    for ci in range(tm // GM_CHUNK):
        rows = slice(ci * GM_CHUNK, (ci + 1) * GM_CHUNK)
        vc = vnb[rows, :]
        s = bs_ref[...]
        for h in range(GM_HEADS):
            s = s + jnp.where(col_head == h, _dot(wm[h], vc), 0.0)
        ogm_ref[rows, :] = _bf(mu[rows, :] * s)


def _mixin(x, cond, npre, w_t, l, kv_earlier, w2p, bg, lnw, lnb, ws, bs, tm, period, seq, tk):
    m = x.shape[0]
    n_seq, per = m // seq, seq // tm
    extra = [] if kv_earlier is None else list(kv_earlier)
    row = lambda i: (i, 0)
    const2 = lambda i: (0, 0)
    rows_out = [(4 * GLA_W, _f32), (GLA_W, _f32), (GM_W, _bf16), (GM_W, _f32), (SB_W, _bf16)]
    t_in = pl.BlockSpec((None, SB_W, tm), lambda i: (i // per, 0, i % per))
    if kv_earlier is None:
        t_spec, t_shape = t_in, jax.ShapeDtypeStruct((n_seq, SB_W, seq), _f32)
    else:
        depth = len(extra) // 2 + 1
        t_spec = pl.BlockSpec((depth, None, SB_W, tm), lambda i: (0, i // per, 0, i % per))
        t_shape = jax.ShapeDtypeStruct((depth, n_seq, SB_W, seq), _f32)
    tb_spec = pl.BlockSpec((None, tm // tk, SB_W, tk), lambda i: (i // per, i % per, 0, 0))
    tb_shape = jax.ShapeDtypeStruct((n_seq, seq // tk, SB_W, tk), _bf16)
    return pl.pallas_call(
        functools.partial(_mixin_kernel, period=period),
        grid=(m // tm,),
        in_specs=[
            pl.BlockSpec((tm, D_MODEL), row),
            cond.spec(1, 0), cond.spec(1, 1),
            pl.BlockSpec((1, D_MODEL), const2),
            _resident((None, MIX_ROWS, D_MODEL), lambda i: (l, 0, 0)),
            pl.BlockSpec((LR_PAD, GLA_W), const2),
            pl.BlockSpec((1, GLA_W), const2),
            pl.BlockSpec((1, GM_W), const2),
            pl.BlockSpec((1, GM_W), const2),
            pl.BlockSpec((GM_HEADS, GM_CHUNK, GM_CHUNK), lambda i: (0, 0, 0)),
            pl.BlockSpec((GM_CHUNK, GM_W), const2),
        ] + [t_in] * len(extra),
        out_specs=[pl.BlockSpec((tm, w), row) for w, _ in rows_out] + [t_spec, t_spec, tb_spec, tb_spec],
        out_shape=[jax.ShapeDtypeStruct((m, w), dt) for w, dt in rows_out]
        + [t_shape, t_shape, tb_shape, tb_shape],
        compiler_params=_params(("parallel",), VMEM_LIMIT),
        name="mixin",
    )(x, cond.arr, cond.arr, npre, w_t, w2p, bg, lnw, lnb, ws, bs, *extra)


def _gla_kernel(gla_ref, la_ref, nw_ref, s0_ref, o_ref, s_ref, st_ref, ob_ref, *, chunk):
    nb, tl = gla_ref.shape[:2]
    j = pl.program_id(1)
    heads = [slice(h * HEAD_DIM, (h + 1) * HEAD_DIM) for h in range(GLA_HEADS)]

    @pl.when(j == 0)
    def _():
        for s in range(nb):
            for h in range(GLA_HEADS):
                st_ref[s, h] = s0_ref[s, h].T

    ri = lax.broadcasted_iota(jnp.int32, (chunk, chunk), 0)
    ci = lax.broadcasted_iota(jnp.int32, (chunk, chunk), 1)
    causal = ci <= ri
    tri = _bf(jnp.where(causal, 1.0, 0.0))
    gi = _div_pow2(lax.broadcasted_iota(jnp.int32, (GLA_W, GLA_W), 0), HEAD_DIM)
    gj = _div_pow2(lax.broadcasted_iota(jnp.int32, (GLA_W, GLA_W), 1), HEAD_DIM)
    group = _bf(jnp.where(gi == gj, 1.0, 0.0))

    n = tl // chunk
    units = [(s, slice(c * chunk, (c + 1) * chunk)) for s in range(nb) for c in range(n)]
    b = [_cumsum_rows(la_ref[s, r, :], tri) for s, r in units]
    q_t, k_t, k_end, decay, v = [], [], [], [], []
    for u, (s, r) in enumerate(units):
        k = gla_ref[s, r, GLA_W:2 * GLA_W]
        b_last = b[u][chunk - 1:chunk, :]
        q_t.append(_bf(gla_ref[s, r, 0:GLA_W] * (HEAD_DIM ** -0.5) * jnp.exp(b[u])))
        k_t.append(_bf(k * jnp.exp(-b[u])))
        k_end.append(_bf(k * jnp.exp(b_last - b[u])))
        decay.append(jnp.exp(b_last))
        v.append(_bf(gla_ref[s, r, 2 * GLA_W:3 * GLA_W]))
    att = [[_bf(jnp.where(causal, _dot_nt(q_t[u][:, hs], k_t[u][:, hs]), 0.0)) for hs in heads]
           for u in range(len(units))]
    kv = [[_dot_tn(v[u][:, hs], k_end[u][:, hs]) for hs in heads] for u in range(len(units))]
    for u, (s, r) in enumerate(units):
        for h, hs in enumerate(heads):
            ob_ref[s, r, hs] = _dot(att[u][h], v[u][:, hs])
    entry = {}
    for s in range(nb):
        for h, hs in enumerate(heads):
            st = st_ref[s, h]
            for c in range(n):
                u = s * n + c
                entry[u, h] = _bf(st)
                st = st * decay[u][:, hs] + kv[u][h]
            st_ref[s, h] = st
    for u, (s, r) in enumerate(units):
        for h, hs in enumerate(heads):
            ob_ref[s, r, hs] += _dot_nt(q_t[u][:, hs], entry[u, h])
    for s, r in units:
        o = ob_ref[s, r, :]
        g = gla_ref[s, r, 3 * GLA_W:4 * GLA_W]
        ms = _dot_split(o * o, group, 2) * (1.0 / HEAD_DIM)
        y = o * lax.rsqrt(ms + EPS) * nw_ref[...]
        o_ref[s, r, :] = _bf(y * (g * _sigmoid(g)))

    @pl.when(j == pl.num_programs(1) - 1)
    def _():
        for s in range(nb):
            for h in range(GLA_HEADS):
                s_ref[s, h] = st_ref[s, h].T


def _gla(gla, la, nw, s0, nb, tl, chunk):
    bsz, length, _ = gla.shape
    state_spec = pl.BlockSpec((nb, GLA_HEADS, HEAD_DIM, HEAD_DIM), lambda b, j: (b, 0, 0, 0))
    return pl.pallas_call(
        functools.partial(_gla_kernel, chunk=chunk),
        grid=(bsz // nb, length // tl),
        in_specs=[
            pl.BlockSpec((nb, tl, 4 * GLA_W), lambda b, j: (b, j, 0)),
            pl.BlockSpec((nb, tl, GLA_W), lambda b, j: (b, j, 0)),
            pl.BlockSpec((1, GLA_W), lambda b, j: (0, 0)),
            state_spec,
        ],
        out_specs=[pl.BlockSpec((nb, tl, GLA_W), lambda b, j: (b, j, 0)), state_spec],
        out_shape=[
            jax.ShapeDtypeStruct((bsz, length, GLA_W), _bf16),
            jax.ShapeDtypeStruct((bsz, GLA_HEADS, HEAD_DIM, HEAD_DIM), _f32),
        ],
        scratch_shapes=[
            pltpu.VMEM((nb, GLA_HEADS, HEAD_DIM, HEAD_DIM), _f32),
            pltpu.VMEM((nb, tl, GLA_W), _f32),
        ],
        compiler_params=_params(("parallel", "arbitrary")),
        name="gla",
    )(gla, la, nw, s0)


def _sb_table(tk, with_total):
    cols = tk + LANES if with_total else tk
    r = lax.broadcasted_iota(jnp.int32, (2 * tk, cols), 0) & (tk - 1)
    c = lax.broadcasted_iota(jnp.int32, (2 * tk, cols), 1)
    return _bf(jnp.where((r >= c) | (c >= tk), 1.0, 0.0))


def _sb_weights(z, table, c_run, mask):
    tk = z.shape[1]
    sp = _softplus(z)
    if mask is not None:
        sp = jnp.where(mask, sp, 0.0)
    sums = _dot(jnp.concatenate(_split_bf16(sp, 2), axis=1), table)
    a = jnp.exp(z - (sums[:, :tk] + c_run))
    if mask is not None:
        a = jnp.where(mask, a, 0.0)
    return _bf(a), sums


def _sbp_kernel(bias_ref, q_ref, k_ref, v_ref, o_ref,
                table_ref, qh_ref, c_ref, acc_ref, z_ref, sp_ref, a_ref):
    tq = q_ref.shape[0]
    assert k_ref.shape[-1] == tq
    i = pl.program_id(1)
    table_ref[...] = _sb_table(tq, with_total=False)
    heads = [slice(h * HEAD_DIM, (h + 1) * HEAD_DIM) for h in range(SB_HEADS)]
    for h, hs in enumerate(heads):
        qh_ref[h] = q_ref[:, hs]

    def visit(j, mask, first):
        for h, hs in enumerate(heads):
            z = _dot(qh_ref[h], k_ref[j, hs, :]) + bias_ref[h]
            sp = _softplus(z)
            if mask is not None:
                sp = jnp.where(mask, sp, 0.0)
            z_ref[h] = z
            sp_ref[h] = jnp.concatenate(_split_bf16(sp, 2), axis=1)
        for h in range(SB_HEADS):
            tail = _dot(sp_ref[h], table_ref[...])
            c_run = jnp.zeros((tq, LANES), _f32) if first else c_ref[h]
            a = jnp.exp(z_ref[h] - (tail + jnp.concatenate([c_run] * (tq // LANES), axis=1)))
            if mask is not None:
                a = jnp.where(mask, a, 0.0)
            a_ref[h] = _bf(a)
            block_sum = jnp.broadcast_to(tail[:, 0:1], (tq, LANES))
            c_ref[h] = block_sum if first else c_run + block_sum
        for h, hs in enumerate(heads):
            av = _dot_nt(a_ref[h], v_ref[j, hs, :])
            acc_ref[h] = av if first else acc_ref[h] + av

    r = lax.broadcasted_iota(jnp.int32, (tq, tq), 0)
    c = lax.broadcasted_iota(jnp.int32, (tq, tq), 1)
    visit(i, c < r, True)

    def body(jj, carry):
        visit(i - 1 - jj, None, False)
        return carry

    lax.fori_loop(0, i, body, 0)
    for h, hs in enumerate(heads):
        o_ref[:, hs] = _bf(acc_ref[h])


def _sb_prompt(qb, ktb, vtb, bias):
    bsz, length, _ = qb.shape
    nblk, _, tq = ktb.shape[1:]
    kv_spec = pl.BlockSpec((None, nblk, SB_W, tq), lambda b, i: (b, 0, 0, 0))
    return pl.pallas_call(
        _sbp_kernel,
        grid=(bsz, nblk),
        in_specs=[
            pl.BlockSpec(memory_space=pltpu.SMEM),
            pl.BlockSpec((None, tq, SB_W), lambda b, i: (b, i, 0)),
            kv_spec, kv_spec,
        ],
        out_specs=pl.BlockSpec((None, tq, SB_W), lambda b, i: (b, i, 0)),
        out_shape=jax.ShapeDtypeStruct((bsz, length, SB_W), _bf16),
        scratch_shapes=[
            pltpu.VMEM((2 * tq, tq), _bf16),
            pltpu.VMEM((SB_HEADS, tq, HEAD_DIM), _bf16),
            pltpu.VMEM((SB_HEADS, tq, LANES), _f32),
            pltpu.VMEM((SB_HEADS, tq, HEAD_DIM), _f32),
            pltpu.VMEM((SB_HEADS, tq, tq), _f32),
            pltpu.VMEM((SB_HEADS, tq, 2 * tq), _bf16),
            pltpu.VMEM((SB_HEADS, tq, tq), _bf16),
        ],
        compiler_params=_params(("parallel", "arbitrary"), VMEM_LIMIT),
        name="sb_prompt",
    )(bias, qb, ktb, vtb)


def _sbs_kernel(pt_ref, bias_ref, qbd_ref, kn_ref, vn_ref, ck_hbm, cv_hbm,
                o_ref, c_ref, acc_ref, z_ref, later_ref, kbuf, vbuf, sems, *, n_new, layer):
    pp, gs, nbuf = SAMPLE_PAGES_PER_STEP, SAMPLE_PAGES_PER_GROUP, SAMPLE_PAGE_BUFFERS
    g = pl.program_id(1)
    n_steps = pl.num_programs(1)
    total = pl.num_programs(0) * n_steps
    step = pl.program_id(0) * n_steps + g

    def page_copies(t, slot):
        bt = t // n_steps
        first_page = (n_steps - 1 - (t - bt * n_steps)) * pp
        copies = []
        for i in range(pp):
            page = pt_ref[bt, first_page + i]
            copies.append(pltpu.make_async_copy(ck_hbm.at[layer, page], kbuf.at[slot, i], sems.at[slot, 0]))
            copies.append(pltpu.make_async_copy(cv_hbm.at[layer, page], vbuf.at[slot, i], sems.at[slot, 1]))
        return copies

    @pl.when(step == 0)
    def _():
        for t in range(nbuf - 1):
            for cp in page_copies(t, t):
                cp.start()

    ahead = step + (nbuf - 1)

    @pl.when(ahead < total)
    def _():
        for cp in page_copies(ahead, ahead % nbuf):
            cp.start()

    slot = step % nbuf
    for cp in page_copies(step, slot):
        cp.wait()
    k_refs = [kbuf.at[slot, i] for i in range(pp)]
    v_refs = [vbuf.at[slot, i] for i in range(pp)]
    rows = qbd_ref.shape[0]
    n = gs * rows
    table = _sb_table(PAGE_SIZE, with_total=True)
    qbd = qbd_ref[...]
    bias = bias_ref[...]

    @pl.when(g == 0)
    def _():
        ri = lax.broadcasted_iota(jnp.int32, (n, n), 0)
        ci = lax.broadcasted_iota(jnp.int32, (n, n), 1)
        same_row = (ri & (rows - 1)) == (ci & (rows - 1))
        later_ref[...] = _bf(jnp.where(same_row & (_div_pow2(ci, rows) > _div_pow2(ri, rows)), 1.0, 0.0))
        t = _div_pow2(lax.broadcasted_iota(jnp.int32, (rows, PAGE_SIZE), 0), SB_HEADS)
        s = lax.broadcasted_iota(jnp.int32, (rows, PAGE_SIZE), 1)
        a, sums = _sb_weights(_dot(qbd, kn_ref[...]) + bias, table,
                                 jnp.zeros((rows, PAGE_SIZE), _f32), s < t)
        acc_ref[...] = _dot_nt(a, vn_ref[...])
        c_ref[...] = sums[:, PAGE_SIZE:]

    groups = [range(k * gs, (k + 1) * gs) for k in reversed(range(pp // gs))]
    for grp in groups:
        for p in grp:
            z_ref[p * rows:(p + 1) * rows, :] = _dot(qbd, _bf(k_refs[p][...])) + bias
    zs, sums = [], []
    for grp in groups:
        zs.append(z_ref[grp[0] * rows:(grp[-1] + 1) * rows, :])
        sums.append(_dot(jnp.concatenate(_split_bf16(_softplus(zs[-1]), 2), axis=1), table))
    c_run = c_ref[...]
    weights = []
    for k in range(len(groups)):
        page_sum = sums[k][:, PAGE_SIZE:]
        nearer = (sum(_dot(later_ref[...], piece) for piece in _split_bf16(page_sum, 2))
                  + jnp.concatenate([c_run] * gs, axis=0))
        weights.append(_bf(jnp.exp(zs[k] - (sums[k][:, :PAGE_SIZE] + nearer))))
        for q in range(gs):
            c_run = c_run + page_sum[q * rows:(q + 1) * rows, :]
    c_ref[...] = c_run
    acc = acc_ref[...]
    for k, grp in enumerate(groups):
        for q, p in enumerate(grp):
            acc = acc + _dot_nt(weights[k][q * rows:(q + 1) * rows, :], _bf(v_refs[p][...]))
    acc_ref[...] = acc

    @pl.when(g == pl.num_programs(1) - 1)
    def _():
        rh = lax.broadcasted_iota(jnp.int32, (rows, SB_W), 0) & (SB_HEADS - 1)
        ch = _div_pow2(lax.broadcasted_iota(jnp.int32, (rows, SB_W), 1), HEAD_DIM)
        own = jnp.where(rh == ch, acc_ref[...], 0.0)
        o_ref[...] = _bf(jnp.sum(own.reshape(n_new, SB_HEADS, SB_W), axis=1))


def _sb_sample(page_table, bias_rows, qbd, knt, vnt, cache_kt, cache_vt, l, n_new):
    db, n_pages = page_table.shape
    pp = SAMPLE_PAGES_PER_STEP
    n_steps = n_pages // pp
    rows = qbd.shape[1]

    grid_spec = pltpu.PrefetchScalarGridSpec(
        num_scalar_prefetch=1,
        grid=(db, n_steps),
        in_specs=[
            pl.BlockSpec((rows, LANES), lambda b, g, pt: (0, 0)),
            pl.BlockSpec((None, rows, SB_W), lambda b, g, pt: (b, 0, 0)),
            pl.BlockSpec((None, SB_W, PAGE_SIZE), lambda b, g, pt: (b, 0, 0)),
            pl.BlockSpec((None, SB_W, PAGE_SIZE), lambda b, g, pt: (b, 0, 0)),
            pl.BlockSpec(memory_space=pl.ANY),
            pl.BlockSpec(memory_space=pl.ANY),
        ],
        out_specs=pl.BlockSpec((None, n_new, SB_W), lambda b, g, pt: (b, 0, 0)),
        scratch_shapes=[
            pltpu.VMEM((rows, LANES), _f32),
            pltpu.VMEM((rows, SB_W), _f32),
            pltpu.VMEM((pp * rows, PAGE_SIZE), _f32),
            pltpu.VMEM((SAMPLE_PAGES_PER_GROUP * rows, SAMPLE_PAGES_PER_GROUP * rows), _bf16),
            pltpu.VMEM((SAMPLE_PAGE_BUFFERS, pp, SB_W, PAGE_SIZE), _f32),
            pltpu.VMEM((SAMPLE_PAGE_BUFFERS, pp, SB_W, PAGE_SIZE), _f32),
            pltpu.SemaphoreType.DMA((SAMPLE_PAGE_BUFFERS, 2)),
        ],
    )
    return pl.pallas_call(
        functools.partial(_sbs_kernel, n_new=n_new, layer=l),
        grid_spec=grid_spec,
        out_shape=jax.ShapeDtypeStruct((db, n_new, SB_W), _bf16),
        compiler_params=_params(("arbitrary", "arbitrary"), VMEM_LIMIT),
        name="sb_sample",
    )(page_table, bias_rows, qbd, knt, vnt, cache_kt, cache_vt)


def _mixout_kernel(x_ref, gt_ref, npost_ref, og_ref, om_ref, os_ref, w_ref, o_ref):
    y = (_dot(og_ref[...], w_ref[0:GLA_W, :])
         + _dot(om_ref[...], w_ref[GLA_W:GLA_W + GM_W, :])
         + _dot(os_ref[...], w_ref[GLA_W + GM_W:, :]))
    o_ref[...] = x_ref[...] + gt_ref[...] * _rms(y, npost_ref[...])


def _mixout(x, cond, npost, o_gla, o_gm, o_sb, w_out, l, tm):
    m = x.shape[0]
    row = lambda i: (i, 0)
    return pl.pallas_call(
        _mixout_kernel,
        grid=(m // tm,),
        in_specs=[
            pl.BlockSpec((tm, D_MODEL), row),
            cond.spec(1, 2),
            pl.BlockSpec((1, D_MODEL), lambda i: (0, 0)),
            pl.BlockSpec((tm, GLA_W), row),
            pl.BlockSpec((tm, GM_W), row),
            pl.BlockSpec((tm, SB_W), row),
            _resident((None, D_MODEL, D_MODEL), lambda i: (l, 0, 0)),
        ],
        out_specs=pl.BlockSpec((tm, D_MODEL), row),
        out_shape=jax.ShapeDtypeStruct((m, D_MODEL), _f32),
        compiler_params=_params(("parallel",), VMEM_LIMIT),
        name="mixout",
    )(x, cond.arr, npost, o_gla, o_gm, o_sb, w_out)


def kernel(x_prompt, x_sample, c_prompt, c_sample, cache_sb_k, cache_sb_v, state_gla, page_table,
           cond_w, cond_b, norm_pre, norm_post, ffn_w_in, ffn_w_out, mix_w_in, mix_w_out,
           gla_w_gate2, gla_b_gate, gla_norm, gm_ln_w, gm_ln_b, gm_ws, gm_bs, sb_bias):
    depth = cond_w.shape[0]
    bsz, seq, _ = x_prompt.shape
    db, n_new, _ = x_sample.shape
    n_pool = cache_sb_k.shape[1]
    mp, ms = bsz * seq, db * n_new
    assert ms == GM_CHUNK and GM_CHUNK % n_new == 0 and seq % TM_PROMPT == 0 and seq % TM_FFN == 0

    w_in_b = _bf(ffn_w_in)
    w_out_b = _bf(ffn_w_out)
    mix_out_b = _bf(mix_w_out)
    mix_t = _bf(jnp.swapaxes(mix_w_in, 1, 2))
    w2_pad = _bf(jnp.pad(gla_w_gate2, ((0, 0), (0, LR_PAD - GLA_LOWRANK), (0, 0))))
    to_t = lambda cch: jnp.transpose(cch, (0, 1, 3, 4, 2)).reshape(depth, n_pool, SB_W, PAGE_SIZE)
    cache_kt, cache_vt = to_t(cache_sb_k), to_t(cache_sb_v)
    head_cols = (jnp.arange(SB_W) // HEAD_DIM)[None, :] == jnp.arange(SB_HEADS)[:, None]

    m_all = _cond(jnp.concatenate([c_prompt, c_sample], axis=0), cond_w, cond_b)

    xp = x_prompt.reshape(mp, D_MODEL)
    xs = x_sample.reshape(ms, D_MODEL)
    s0_prompt = jnp.zeros((bsz, GLA_HEADS, HEAD_DIM, HEAD_DIM), _f32)
    gla_p, gla_s, gms = [], [], []
    kv_p, kv_s = [], []
    vec = lambda a: a.reshape(1, -1)
    from_t = lambda t: jnp.transpose(t.reshape(depth, t.shape[1], SB_HEADS, HEAD_DIM, -1), (0, 1, 4, 2, 3))

    for l in range(depth):
        npre = [vec(norm_pre[l, i]) for i in range(3)]
        npost = [vec(norm_post[l, i]) for i in range(3)]
        bg, nw = vec(gla_b_gate[l]), vec(gla_norm[l])
        lnw, lnb = vec(gm_ln_w[l]), vec(gm_ln_b[l])

        cond = _Cond(m_all[l, :bsz], seq, TM_PROMPT)
        cond_ffn = _Cond(m_all[l, :bsz], seq, TM_FFN)
        xp = _ffn(xp, cond_ffn, 0, npre[0], npost[0], w_in_b, w_out_b, l, 0, TM_FFN)
        bs_rows = jnp.repeat(gm_bs[l].T, HEAD_DIM, axis=1)
        last = l == depth - 1
        gla, la, o_gm, _, qb, kt, vt, ktb, vtb = _mixin(
            xp, cond, npre[1], mix_t, l, kv_p if last else None, w2_pad[l], bg, lnw, lnb, gm_ws[l], bs_rows,
            TM_PROMPT, GM_CHUNK, seq, SB_TQ)
        o_gla, s_gla = _gla(gla.reshape(bsz, seq, 4 * GLA_W), la.reshape(bsz, seq, GLA_W), nw,
                            s0_prompt, 1, GLA_TL, GLA_CHUNK)
        o_sb = _sb_prompt(qb.reshape(bsz, seq, SB_W), ktb, vtb, sb_bias[l])
        xp = _mixout(xp, cond, npost[1], o_gla.reshape(mp, GLA_W), o_gm, o_sb.reshape(mp, SB_W),
                     mix_out_b, l, TM_PROMPT)
        xp = _ffn(xp, cond_ffn, 2, npre[2], npost[2], w_in_b, w_out_b, l, 1, TM_FFN)
        gla_p.append(s_gla)
        kv_p += [kt, vt]

        cond = _Cond(m_all[l, bsz:], n_new, ms)
        xs = _ffn(xs, cond, 0, npre[0], npost[0], w_in_b, w_out_b, l, 0, ms)
        reps = GM_CHUNK // n_new
        ws_s = jnp.tile(gm_ws[l][:, :n_new, :n_new], (1, reps, reps))
        bs_s = jnp.tile(bs_rows[:n_new], (reps, 1))
        gla, la, o_gm, vn, qb, kt, vt, ktb, vtb = _mixin(
            xs, cond, npre[1], mix_t, l, kv_s if last else None, w2_pad[l], bg, lnw, lnb, ws_s, bs_s,
            ms, n_new, ms, ms)
        pad_rows = lambda a, n: jnp.pad(a.reshape(db, n_new, -1), ((0, 0), (0, n - n_new), (0, 0)))
        o_gla, s_gla = _gla(pad_rows(gla, SUBLANES), pad_rows(la, SUBLANES), nw, state_gla[l],
                            GLA_SAMPLE_SEQS, SUBLANES, SUBLANES)
        qbd = (qb.reshape(db, n_new, 1, SB_W) * head_cols.astype(_bf16)[None, None]
               ).reshape(db, n_new * SB_HEADS, SB_W)
        bias_rows = jnp.broadcast_to(jnp.tile(sb_bias[l], n_new)[:, None], (n_new * SB_HEADS, LANES))
        new_page = lambda t: jnp.pad(jnp.transpose(t.reshape(SB_W, db, n_new), (1, 0, 2)),
                                     ((0, 0), (0, 0), (0, PAGE_SIZE - n_new)))
        o_sb = _sb_sample(page_table, bias_rows, qbd, new_page(ktb), new_page(vtb),
                          cache_kt, cache_vt, l, n_new)
        xs = _mixout(xs, cond, npost[1], o_gla[:, :n_new].reshape(ms, GLA_W), o_gm,
                     o_sb.reshape(ms, SB_W), mix_out_b, l, ms)
        xs = _ffn(xs, cond, 2, npre[2], npost[2], w_in_b, w_out_b, l, 1, ms)
        gla_s.append(s_gla)
        kv_s += [kt, vt]
        gms.append(vn.reshape(db, n_new, GM_W))

    kp, vp = (from_t(t) for t in kv_p[-2:])
    ksl, vsl = (from_t(t).reshape(depth, db, n_new, SB_HEADS, HEAD_DIM) for t in kv_s[-2:])
    return (xp.reshape(bsz, seq, D_MODEL), xs.reshape(db, n_new, D_MODEL),
            jnp.stack(gla_p), jnp.stack(gla_s), kp, vp, ksl, vsl, jnp.stack(gms))
```
